```python
import math
import jax, jax.numpy as jnp
from jax import lax
import numpy as np

D_MODEL = 1024
BATCH = 2
SEQ = 8192
DEPTH = 2
DEC_BATCH = 128
DEC_SEQ = 1
PAST_LEN = 2048
PAGE_SIZE = 128

N_A_LAYERS = DEPTH // 2
N_B_LAYERS = DEPTH - N_A_LAYERS
D_RNN = D_MODEL
N_LRU_BLOCKS = 4
LRU_BLOCK = D_RNN // N_LRU_BLOCKS
CONV_W = 4
LRU_C = 8.0
N_HEADS = 16
HEAD_DIM = D_MODEL // N_HEADS
MOBA_BLOCK = 256
MOBA_TOPK = 3
Q_CHUNK = 64
D_FF = 4 * D_MODEL
EPS = 1e-6
NEG = -1e30

kernel_name = "yoco_rglru_moba_decoder_step"


def rmsnorm(x, g):
    xf = x.astype(jnp.float32)
    y = xf * lax.rsqrt(jnp.mean(xf * xf, axis=-1, keepdims=True) + EPS) * g.astype(jnp.float32)
    return y.astype(x.dtype)


def sqrelu_mlp(x, w1, w2):
    h = jax.nn.relu(x @ w1)
    return (h * h) @ w2


def causal_conv(u, w, b):
    T = u.shape[1] - (CONV_W - 1)
    out = b + u[:, 0:T] * w[0]
    for k in range(1, CONV_W):
        out = out + u[:, k:k + T] * w[k]
    return out


def rg_lru(x, h0, w_a, b_a, w_i, b_i, lam):
    Bn, T, _ = x.shape
    xb = x.reshape(Bn, T, N_LRU_BLOCKS, LRU_BLOCK)
    r = jax.nn.sigmoid(jnp.einsum('btnc,ncd->btnd', xb, w_a) + b_a).reshape(Bn, T, D_RNN)
    i = jax.nn.sigmoid(jnp.einsum('btnc,ncd->btnd', xb, w_i) + b_i).reshape(Bn, T, D_RNN)
    log_a = -LRU_C * r.astype(jnp.float32) * jax.nn.softplus(-lam.astype(jnp.float32))
    a = jnp.exp(log_a)
    mult = jnp.sqrt(-jnp.expm1(2.0 * log_a))
    bterm = mult * (i.astype(jnp.float32) * x.astype(jnp.float32))

    def step(h, ab):
        a_t, b_t = ab
        h = a_t * h + b_t
        return h, h

    h_last, hs = lax.scan(step, h0.astype(jnp.float32),
                          (jnp.swapaxes(a, 0, 1), jnp.swapaxes(bterm, 0, 1)))
    return jnp.swapaxes(hs, 0, 1).astype(x.dtype), h_last.astype(x.dtype)


def recurrent_block(xn, conv_buf, h0, w_in, b_in, conv_w, conv_b,
                    w_ga, b_ga, w_gi, b_gi, lam, w_out, b_out):
    proj = xn @ w_in + b_in
    gate, u = proj[..., :D_RNN], proj[..., D_RNN:]
    gate = jax.nn.gelu(gate)
    full = jnp.concatenate([conv_buf.astype(u.dtype), u], axis=1)
    new_buf = full[:, -(CONV_W - 1):]
    c = causal_conv(full, conv_w, conv_b)
    h, h_last = rg_lru(c, h0, w_ga, b_ga, w_gi, b_gi, lam)
    return (h * gate) @ w_out + b_out, new_buf, h_last


def moba_attention(q, k, v, q_pos):
    Bn, Lq = q.shape[:2]
    Lk = k.shape[1]
    pad = (-Lk) % MOBA_BLOCK
    k = jnp.pad(k, ((0, 0), (0, pad), (0, 0), (0, 0)))
    v = jnp.pad(v, ((0, 0), (0, pad), (0, 0), (0, 0)))
    NB = (Lk + pad) // MOBA_BLOCK
    kb = k.reshape(Bn, NB, MOBA_BLOCK, N_HEADS, HEAD_DIM).transpose(0, 3, 1, 2, 4)
    vb = v.reshape(Bn, NB, MOBA_BLOCK, N_HEADS, HEAD_DIM).transpose(0, 3, 1, 2, 4)
    k_mean = jnp.mean(kb.astype(jnp.float32), axis=3)
    chunk = math.gcd(Lq, Q_CHUNK)
    n_chunks = Lq // chunk
    qc = q.reshape(Bn, n_chunks, chunk, N_HEADS, HEAD_DIM).transpose(1, 0, 2, 3, 4)
    pc = q_pos.reshape(n_chunks, chunk)
    bi = jnp.arange(Bn)[:, None, None, None]
    hi = jnp.arange(N_HEADS)[None, None, :, None]
    scale = HEAD_DIM ** -0.5

    def one_chunk(args):
        qi, pos = args
        qblk = pos // MOBA_BLOCK
        gate = jnp.einsum('bchd,bhnd->bchn', qi.astype(jnp.float32), k_mean)
        past = jnp.arange(NB)[None, :] < qblk[:, None]
        gate = jnp.where(past[None, :, None, :], gate, NEG)
        if NB < MOBA_TOPK:
            gate = jnp.pad(gate, ((0, 0), (0, 0), (0, 0), (0, MOBA_TOPK - NB)), constant_values=NEG)
        _, top = lax.top_k(gate, MOBA_TOPK)
        top = jnp.minimum(top, NB - 1)
        own = jnp.broadcast_to(qblk[None, :, None, None], top.shape[:3] + (1,))
        idx = jnp.concatenate([top, own.astype(top.dtype)], axis=-1)
        slot_ok = jnp.concatenate([jnp.arange(MOBA_TOPK)[None, :] < qblk[:, None],
                                   jnp.ones((qblk.shape[0], 1), bool)], axis=-1)
        kg = kb[bi, hi, idx]
        vg = vb[bi, hi, idx]
        key_pos = idx[..., None] * MOBA_BLOCK + jnp.arange(MOBA_BLOCK)
        mask = slot_ok[None, :, None, :, None] & (key_pos <= pos[None, :, None, None, None])
        s = jnp.einsum('bchd,bchskd->bchsk', qi, kg).astype(jnp.float32) * scale
        s = jnp.where(mask, s, NEG)
        sh = s.shape
        p = jax.nn.softmax(s.reshape(sh[:3] + (-1,)), axis=-1).reshape(sh).astype(vg.dtype)
        return jnp.einsum('bchsk,bchskd->bchd', p, vg)

    out = lax.map(one_chunk, (qc, pc))
    return out.transpose(1, 0, 2, 3, 4).reshape(Bn, Lq, N_HEADS * HEAD_DIM)


def setup_inputs(seed: int = 0) -> dict:
    key = jax.random.key(seed)
    ks = jax.random.split(key, 32)
    f32 = jnp.float32
    n_pages = PAST_LEN // PAGE_SIZE
    n_used = DEC_BATCH * n_pages
    n_phys = n_used + (n_used + 3) // 4

    def nrm(k, shape, scale):
        return jax.random.normal(k, shape, f32) * scale

    u = jax.random.uniform(ks[20], (N_A_LAYERS, D_RNN), f32, 0.9, 0.999)
    s = u ** (1.0 / LRU_C)
    lru_lambda = jnp.log(s) - jnp.log1p(-s)
    return {
        "x_prompt": nrm(ks[0], (BATCH, SEQ, D_MODEL), 1.0),
        "x_sample": nrm(ks[1], (DEC_BATCH, DEC_SEQ, D_MODEL), 1.0),
        "state_conv": nrm(ks[2], (N_A_LAYERS, DEC_BATCH, CONV_W - 1, D_RNN), 1.0),
        "state_h": nrm(ks[3], (N_A_LAYERS, DEC_BATCH, D_RNN), 0.5),
        "cache_k": nrm(ks[4], (n_phys, PAGE_SIZE, N_HEADS, HEAD_DIM), 1.0),
        "cache_v": nrm(ks[5], (n_phys, PAGE_SIZE, N_HEADS, HEAD_DIM), 1.0),
        "page_table": jax.random.permutation(ks[6], n_phys)[:n_used].reshape(DEC_BATCH, n_pages).astype(jnp.int32),
        "norm_mix": 1.0 + nrm(ks[7], (DEPTH, D_MODEL), 0.05),
        "norm_mlp": 1.0 + nrm(ks[8], (DEPTH, D_MODEL), 0.05),
        "w_ff1": nrm(ks[9], (DEPTH, D_MODEL, D_FF), D_MODEL ** -0.5),
        "w_ff2": nrm(ks[10], (DEPTH, D_FF, D_MODEL), D_FF ** -0.5),
        "w_rg_in": nrm(ks[11], (N_A_LAYERS, D_MODEL, 2 * D_RNN), D_MODEL ** -0.5),
        "b_rg_in": nrm(ks[12], (N_A_LAYERS, 2 * D_RNN), 0.01),
        "conv_w": nrm(ks[13], (N_A_LAYERS, CONV_W, D_RNN), CONV_W ** -0.5),
        "conv_b": nrm(ks[14], (N_A_LAYERS, D_RNN), 0.01),
        "w_gate_a": nrm(ks[15], (N_A_LAYERS, N_LRU_BLOCKS, LRU_BLOCK, LRU_BLOCK), LRU_BLOCK ** -0.5),
        "b_gate_a": nrm(ks[16], (N_A_LAYERS, N_LRU_BLOCKS, LRU_BLOCK), 0.01),
        "w_gate_i": nrm(ks[17], (N_A_LAYERS, N_LRU_BLOCKS, LRU_BLOCK, LRU_BLOCK), LRU_BLOCK ** -0.5),
        "b_gate_i": nrm(ks[18], (N_A_LAYERS, N_LRU_BLOCKS, LRU_BLOCK), 0.01),
        "lru_lambda": lru_lambda,
        "w_rg_out": nrm(ks[21], (N_A_LAYERS, D_RNN, D_MODEL), D_RNN ** -0.5),
        "b_rg_out": nrm(ks[22], (N_A_LAYERS, D_MODEL), 0.01),
        "norm_kv": 1.0 + nrm(ks[23], (D_MODEL,), 0.05),
        "w_kv": nrm(ks[24], (D_MODEL, 2 * N_HEADS * HEAD_DIM), D_MODEL ** -0.5),
        "w_q": nrm(ks[25], (N_B_LAYERS, D_MODEL, N_HEADS * HEAD_DIM), D_MODEL ** -0.5),
        "w_o": nrm(ks[26], (N_B_LAYERS, N_HEADS * HEAD_DIM, D_MODEL), D_MODEL ** -0.5),
        "norm_out": 1.0 + nrm(ks[27], (D_MODEL,), 0.05),
    }


def reference(x_prompt, x_sample, state_conv, state_h, cache_k, cache_v, page_table,
              norm_mix, norm_mlp, w_ff1, w_ff2, w_rg_in, b_rg_in, conv_w, conv_b,
              w_gate_a, b_gate_a, w_gate_i, b_gate_i, lru_lambda, w_rg_out, b_rg_out,
              norm_kv, w_kv, w_q, w_o, norm_out):
    HD = N_HEADS * HEAD_DIM

    def trunk(x, conv0, h0, past_k, past_v):
        Bn, L, _ = x.shape
        q_pos = past_k.shape[1] + jnp.arange(L, dtype=jnp.int32)
        new_conv, new_h = [], []
        k_all = v_all = k_new = v_new = None
        for layer in range(DEPTH):
            hn = rmsnorm(x, norm_mix[layer])
            if layer < N_A_LAYERS:
                a = layer
                out, buf, hl = recurrent_block(hn, conv0[a], h0[a], w_rg_in[a], b_rg_in[a],
                                               conv_w[a], conv_b[a], w_gate_a[a], b_gate_a[a],
                                               w_gate_i[a], b_gate_i[a], lru_lambda[a],
                                               w_rg_out[a], b_rg_out[a])
                new_conv.append(buf)
                new_h.append(hl)
            else:
                b = layer - N_A_LAYERS
                q = (hn @ w_q[b]).reshape(Bn, L, N_HEADS, HEAD_DIM)
                out = moba_attention(q, k_all, v_all, q_pos) @ w_o[b]
            x = x + out
            x = x + sqrelu_mlp(rmsnorm(x, norm_mlp[layer]), w_ff1[layer], w_ff2[layer])
            if layer == N_A_LAYERS - 1:
                kv = rmsnorm(x, norm_kv) @ w_kv
                k_new = kv[..., :HD].reshape(Bn, L, N_HEADS, HEAD_DIM)
                v_new = kv[..., HD:].reshape(Bn, L, N_HEADS, HEAD_DIM)
                k_all = jnp.concatenate([past_k.astype(k_new.dtype), k_new], axis=1)
                v_all = jnp.concatenate([past_v.astype(v_new.dtype), v_new], axis=1)
        return rmsnorm(x, norm_out), jnp.stack(new_conv), jnp.stack(new_h), k_new, v_new

    dt = x_prompt.dtype
    empty = jnp.zeros((BATCH, 0, N_HEADS, HEAD_DIM), dt)
    y_prompt, conv_p, h_p, k_p, v_p = trunk(
        x_prompt,
        jnp.zeros((N_A_LAYERS, BATCH, CONV_W - 1, D_RNN), dt),
        jnp.zeros((N_A_LAYERS, BATCH, D_RNN), dt),
        empty, empty)

    n_pages = PAST_LEN // PAGE_SIZE
    past_k = cache_k[page_table].reshape(DEC_BATCH, n_pages * PAGE_SIZE, N_HEADS, HEAD_DIM)
    past_v = cache_v[page_table].reshape(DEC_BATCH, n_pages * PAGE_SIZE, N_HEADS, HEAD_DIM)
    y_sample, conv_s, h_s, k_s, v_s = trunk(x_sample, state_conv, state_h, past_k, past_v)

    return (y_prompt, y_sample, conv_p, h_p, k_p, v_p, conv_s, h_s, k_s, v_s)
```

```python
import functools
import math

import jax
import jax.numpy as jnp
from jax import lax
from jax.experimental import pallas as pl
from jax.experimental.pallas import tpu as pltpu

N_HEADS = 16
HEAD_DIM = 64
N_LRU_BLOCKS = 4
CONV_W = 4
LRU_C = 8.0
MOBA_BLOCK = 256
MOBA_TOPK = 3
EPS = 1e-6
NEG = -1e30

LANES = 128
SUBLANES = 8
HEADS_PER_GROUP = LANES // HEAD_DIM
VMEM_LIMIT = 56 * 1024 * 1024

BF16 = jnp.bfloat16
F32 = jnp.float32


def _rmsnorm(x, g):
    return x * lax.rsqrt(jnp.mean(x * x, axis=-1, keepdims=True) + EPS) * g


def _dot(a, b, precision=None):
    return jnp.dot(a, b, preferred_element_type=F32, precision=precision)


def _dot_nt(a, b):
    return lax.dot_general(a, b, (((1,), (1,)), ((), ())), preferred_element_type=F32)


def _gelu_tanh(x):
    c = math.sqrt(2.0 / math.pi)
    return x * (0.5 * (1.0 + jnp.tanh(c * (x + 0.044715 * (x * x * x)))))


def _softplus(y):
    return jnp.maximum(y, 0.0) + jnp.log1p(jnp.exp(-jnp.abs(y)))


def _neg_expm1(x):
    return jnp.tanh(-0.5 * x) * (jnp.exp(x) + 1.0)


def _lru_coeffs(c, wga_ref, bga, wgi_ref, bgi, lam):
    blk = c.shape[1] // N_LRU_BLOCKS
    cb = c.astype(BF16)
    ra, ri = [], []
    for n in range(N_LRU_BLOCKS):
        cn = cb[:, n * blk:(n + 1) * blk]
        ra.append(_dot(cn, wga_ref[n]))
        ri.append(_dot(cn, wgi_ref[n]))
    r = jax.nn.sigmoid(jnp.concatenate(ra, axis=1) + bga)
    i = jax.nn.sigmoid(jnp.concatenate(ri, axis=1) + bgi)
    log_a = -LRU_C * r * _softplus(-lam)
    a = jnp.exp(log_a)
    mult = jnp.sqrt(_neg_expm1(2.0 * log_a))
    return a, mult * (i * c)


def _rglru_seq_kernel(x_ref, conv0_ref, h0_ref, g_ref, win_ref, bin_ref, cw_ref, cb_ref,
                      wga_ref, bga_ref, wgi_ref, bgi_ref, lam_ref, wout_ref, bout_ref,
                      x1_ref, convn_ref, hn_ref,
                      ubuf, aloc, hloc, hcar, *, tm, d_rnn):
    t = pl.program_id(1)
    nt = pl.num_programs(1)
    hist = CONV_W - 1

    @pl.when(t == 0)
    def _():
        ubuf[SUBLANES - hist:SUBLANES, :] = conv0_ref[0]
        hcar[...] = h0_ref[0]

    x = x_ref[...]
    xn = _rmsnorm(x, g_ref[...]).astype(BF16)
    proj = _dot(xn, win_ref[...]) + bin_ref[...]
    gate = _gelu_tanh(proj[:, :d_rnn])
    u = proj[:, d_rnn:]
    ubuf[SUBLANES:SUBLANES + tm, :] = u

    c = cb_ref[...] + ubuf[SUBLANES - hist:SUBLANES - hist + tm, :] * cw_ref[0:1, :]
    for k in range(1, CONV_W):
        off = SUBLANES - hist + k
        c = c + ubuf[off:off + tm, :] * cw_ref[k:k + 1, :]

    a, b = _lru_coeffs(c, wga_ref, bga_ref[...], wgi_ref, bgi_ref[...], lam_ref[...])

    row = lax.broadcasted_iota(jnp.int32, a.shape, 0) % SUBLANES
    for s in (1, 2, 4):
        ok = row >= s
        a_sh = pltpu.roll(a, s, 0)
        b_sh = pltpu.roll(b, s, 0)
        b = jnp.where(ok, a * b_sh + b, b)
        a = jnp.where(ok, a * a_sh, a)
    aloc[...] = a
    hloc[...] = b

    def body(gi, carry):
        r0 = pl.multiple_of(gi * SUBLANES, SUBLANES)
        h = hloc[pl.ds(r0, SUBLANES), :] + aloc[pl.ds(r0, SUBLANES), :] * carry
        hloc[pl.ds(r0, SUBLANES), :] = h
        return h[SUBLANES - 1:SUBLANES, :]

    hcar[...] = lax.fori_loop(0, tm // SUBLANES, body, hcar[...])

    hg = (hloc[...] * gate).astype(BF16)
    x1_ref[...] = x + _dot(hg, wout_ref[...]) + bout_ref[...]

    ubuf[SUBLANES - hist:SUBLANES, :] = ubuf[SUBLANES + tm - hist:SUBLANES + tm, :]

    @pl.when(t == nt - 1)
    def _():
        convn_ref[0] = ubuf[SUBLANES - hist:SUBLANES, :]
        hn_ref[0] = hcar[...]


def _rglru_seq(x, conv0, h0, g, w_in, b_in, cw, cb, wga, bga, wgi, bgi, lam, w_out, b_out, *, tm):
    B, T, D = x.shape
    d_rnn = w_in.shape[1] // 2
    hist = CONV_W - 1
    assert T % tm == 0 and tm % SUBLANES == 0 and tm >= hist
    nt = T // tm
    x2 = x.reshape(B * T, D)
    full = lambda a: pl.BlockSpec(a.shape, lambda b, t: (0,) * a.ndim)
    row = lambda v: v.reshape(1, -1)
    args = (x2, conv0, h0.reshape(B, 1, d_rnn), row(g), w_in, row(b_in), cw, row(cb),
            wga, row(bga), wgi, row(bgi), row(lam), w_out, row(b_out))
    in_specs = [pl.BlockSpec((tm, D), lambda b, t: (b * nt + t, 0)),
                pl.BlockSpec((1, hist, d_rnn), lambda b, t: (b, 0, 0)),
                pl.BlockSpec((1, 1, d_rnn), lambda b, t: (b, 0, 0))]
    in_specs += [full(a) for a in args[3:]]
    x1, convn, hn = pl.pallas_call(
        functools.partial(_rglru_seq_kernel, tm=tm, d_rnn=d_rnn),
        grid=(B, nt),
        in_specs=in_specs,
        out_specs=[pl.BlockSpec((tm, D), lambda b, t: (b * nt + t, 0)),
                   pl.BlockSpec((1, hist, d_rnn), lambda b, t: (b, 0, 0)),
                   pl.BlockSpec((1, 1, d_rnn), lambda b, t: (b, 0, 0))],
        out_shape=[jax.ShapeDtypeStruct((B * T, D), F32),
                   jax.ShapeDtypeStruct((B, hist, d_rnn), F32),
                   jax.ShapeDtypeStruct((B, 1, d_rnn), F32)],
        scratch_shapes=[pltpu.VMEM((tm + SUBLANES, d_rnn), F32),
                        pltpu.VMEM((tm, d_rnn), F32),
                        pltpu.VMEM((tm, d_rnn), F32),
                        pltpu.VMEM((1, d_rnn), F32)],
        compiler_params=pltpu.CompilerParams(
            dimension_semantics=("arbitrary", "arbitrary"), vmem_limit_bytes=VMEM_LIMIT),
        name="rglru_seq",
    )(*args)
    return x1, convn, hn.reshape(B, d_rnn)


def _rglru_step_kernel(x_ref, conv_ref, h0_ref, g_ref, win_ref, bin_ref, cw_ref, cb_ref,
                       wga_ref, bga_ref, wgi_ref, bgi_ref, lam_ref, wout_ref, bout_ref,
                       x1_ref, convn_ref, hn_ref, *, d_rnn):
    hist = CONV_W - 1
    x = x_ref[...]
    xn = _rmsnorm(x, g_ref[...]).astype(BF16)
    proj = _dot(xn, win_ref[...]) + bin_ref[...]
    gate = _gelu_tanh(proj[:, :d_rnn])
    u = proj[:, d_rnn:]
    c = cb_ref[...] + conv_ref[0] * cw_ref[0:1, :]
    for k in range(1, hist):
        c = c + conv_ref[k] * cw_ref[k:k + 1, :]
    c = c + u * cw_ref[hist:hist + 1, :]
    a, b = _lru_coeffs(c, wga_ref, bga_ref[...], wgi_ref, bgi_ref[...], lam_ref[...])
    h = a * h0_ref[...] + b
    hn_ref[...] = h
    for k in range(hist - 1):
        convn_ref[k] = conv_ref[k + 1]
    convn_ref[hist - 1] = u
    x1_ref[...] = x + _dot((h * gate).astype(BF16), wout_ref[...]) + bout_ref[...]


def _rglru_step(x, conv0, h0, g, w_in, b_in, cw, cb, wga, bga, wgi, bgi, lam, w_out, b_out):
    N, D = x.shape
    d_rnn = w_in.shape[1] // 2
    hist = CONV_W - 1
    row = lambda v: v.reshape(1, -1)
    args = (x, conv0, h0, row(g), w_in, row(b_in), cw, row(cb),
            wga, row(bga), wgi, row(bgi), row(lam), w_out, row(b_out))
    return pl.pallas_call(
        functools.partial(_rglru_step_kernel, d_rnn=d_rnn),
        out_shape=[jax.ShapeDtypeStruct((N, D), F32),
                   jax.ShapeDtypeStruct((hist, N, d_rnn), F32),
                   jax.ShapeDtypeStruct((N, d_rnn), F32)],
        compiler_params=pltpu.CompilerParams(vmem_limit_bytes=VMEM_LIMIT),
        name="rglru_step",
    )(*args)


def _sqrelu_mlp(xn_bf, w1_ref, w2_ref, fc):
    d_ff = w1_ref.shape[1]
    acc = None
    for c0 in range(0, d_ff, fc):
        h = jnp.maximum(_dot(xn_bf, w1_ref[:, c0:c0 + fc]), 0.0)
        part = _dot((h * h).astype(BF16), w2_ref[c0:c0 + fc, :])
        acc = part if acc is None else acc + part
    return acc


def _mlp_kv_kernel(x_ref, gm_ref, w1_ref, w2_ref, gkv_ref, wkvt_ref, gq_ref, wq_ref,
                   x2_ref, kt_ref, vt_ref, q_ref, *, fc):
    x = x_ref[...]
    x2 = x + _sqrelu_mlp(_rmsnorm(x, gm_ref[...]).astype(BF16), w1_ref, w2_ref, fc)
    x2_ref[...] = x2
    hd = kt_ref.shape[1]
    kvt = _dot_nt(wkvt_ref[...], _rmsnorm(x2, gkv_ref[...]).astype(BF16))
    kt_ref[0] = kvt[:hd, :]
    vt_ref[0] = kvt[hd:, :]
    q_ref[...] = _dot(_rmsnorm(x2, gq_ref[...]).astype(BF16), wq_ref[...])


def _mlp_kv(x, B, gm, w1, w2, gkv, wkvt, gq, wq, *, tm, fc=512):
    N, D = x.shape
    hd = wq.shape[1]
    T = N // B
    assert N == B * T and T % tm == 0
    nt = T // tm
    row = lambda v: v.reshape(1, -1)
    args = (x, row(gm), w1, w2, row(gkv), wkvt, row(gq), wq)
    full = lambda a: pl.BlockSpec(a.shape, lambda i: (0,) * a.ndim)
    tile = lambda w: pl.BlockSpec((tm, w), lambda i: (i, 0))
    tile_t = pl.BlockSpec((1, hd, tm), lambda i: (i // nt, 0, i % nt))
    return pl.pallas_call(
        functools.partial(_mlp_kv_kernel, fc=fc),
        grid=(N // tm,),
        in_specs=[tile(D)] + [full(a) for a in args[1:]],
        out_specs=[tile(D), tile_t, tile_t, tile(hd)],
        out_shape=[jax.ShapeDtypeStruct((N, D), F32),
                   jax.ShapeDtypeStruct((B, hd, T), F32),
                   jax.ShapeDtypeStruct((B, hd, T), F32),
                   jax.ShapeDtypeStruct((N, hd), F32)],
        compiler_params=pltpu.CompilerParams(
            dimension_semantics=("arbitrary",), vmem_limit_bytes=VMEM_LIMIT),
        name="mlp_kv",
    )(*args)


def _attn_out_mlp_kernel(x_ref, a_ref, wo_ref, gm_ref, w1_ref, w2_ref, go_ref, y_ref, *, fc):
    x3 = x_ref[...] + _dot(a_ref[...].astype(BF16), wo_ref[...])
    x4 = x3 + _sqrelu_mlp(_rmsnorm(x3, gm_ref[...]).astype(BF16), w1_ref, w2_ref, fc)
    y_ref[...] = _rmsnorm(x4, go_ref[...])


def _attn_out_mlp(x, attn, wo, gm, w1, w2, go, *, tm, fc=512):
    N, D = x.shape
    assert N % tm == 0
    row = lambda v: v.reshape(1, -1)
    args = (x, attn, wo, row(gm), w1, w2, row(go))
    full = lambda a: pl.BlockSpec(a.shape, lambda i: (0,) * a.ndim)
    tile = lambda w: pl.BlockSpec((tm, w), lambda i: (i, 0))
    return pl.pallas_call(
        functools.partial(_attn_out_mlp_kernel, fc=fc),
        grid=(N // tm,),
        in_specs=[tile(D), tile(attn.shape[1])] + [full(a) for a in args[2:]],
        out_specs=tile(D),
        out_shape=jax.ShapeDtypeStruct((N, D), F32),
        compiler_params=pltpu.CompilerParams(
            dimension_semantics=("arbitrary",), vmem_limit_bytes=VMEM_LIMIT),
        name="attn_out_mlp",
    )(*args)


def _topk_select(gate, allowed):
    nb = gate.shape[1]
    blk = lax.broadcasted_iota(jnp.int32, gate.shape, 1)
    g = jnp.where(allowed, gate, NEG)
    sel = jnp.zeros(gate.shape, jnp.bool_)
    for _ in range(MOBA_TOPK):
        m = jnp.max(g, axis=1, keepdims=True)
        first = jnp.min(jnp.where(g == m, blk, nb), axis=1, keepdims=True)
        pick = blk == first
        sel = jnp.logical_or(sel, pick)
        g = jnp.where(pick, -jnp.inf, g)
    return jnp.logical_and(sel, allowed)


def _block_sums(x, nb):
    blk = lax.broadcasted_iota(jnp.int32, (x.shape[0], nb), 1)
    out = jnp.zeros((x.shape[0], nb), F32)
    for j in range(nb):
        sj = jnp.sum(x[:, j * MOBA_BLOCK:(j + 1) * MOBA_BLOCK], axis=1, keepdims=True)
        out = jnp.where(blk == j, sj, out)
    return out


def _moba_seq_kernel(q_ref, kt_ref, vt_ref, o_ref, ktbf, vtbf, kmean_t, *, nb, scale):
    qi = pl.program_id(2)
    blk_rows = MOBA_BLOCK

    @pl.when(qi == 0)
    def _():
        for j in range(nb):
            ktbf[j] = kt_ref[0, :, j * blk_rows:(j + 1) * blk_rows].astype(BF16)
            vtbf[j] = vt_ref[0, :, j * blk_rows:(j + 1) * blk_rows].astype(BF16)
        kmean_t[...] = _block_sums(kt_ref[0], nb) * (1.0 / blk_rows)

    q2 = q_ref[...]
    lane = lax.broadcasted_iota(jnp.int32, q2.shape, 1)
    blk_id = lax.broadcasted_iota(jnp.int32, (blk_rows, nb), 1)
    r_id = lax.broadcasted_iota(jnp.int32, (blk_rows, blk_rows), 0)
    c_id = lax.broadcasted_iota(jnp.int32, (blk_rows, blk_rows), 1)
    k_own = ktbf[qi]
    v_own = vtbf[qi]

    outs = []
    for hh in range(HEADS_PER_GROUP):
        in_head = jnp.logical_and(lane >= hh * HEAD_DIM, lane < (hh + 1) * HEAD_DIM)
        qh = jnp.where(in_head, q2, 0.0)
        gate = _dot(qh, kmean_t[...], precision=lax.Precision.HIGHEST)
        sel = _topk_select(gate, blk_id < qi)
        bias = jnp.where(sel, 0.0, NEG)
        qs = (qh * scale).astype(BF16)

        s = jnp.where(c_id <= r_id, _dot(qs, k_own), NEG)
        m = jnp.max(s, axis=1, keepdims=True)
        p = jnp.exp(s - m)
        l = jnp.sum(p, axis=1, keepdims=True)
        acc = _dot_nt(p.astype(BF16), v_own)

        def body(j, carry):
            m, l, acc = carry
            col = jnp.sum(jnp.where(blk_id == j, bias, 0.0), axis=1, keepdims=True)
            s = _dot(qs, ktbf[j]) + col
            m_new = jnp.maximum(m, jnp.max(s, axis=1, keepdims=True))
            alpha = jnp.exp(m - m_new)
            p = jnp.exp(s - m_new)
            l = alpha * l + jnp.sum(p, axis=1, keepdims=True)
            acc = alpha * acc + _dot_nt(p.astype(BF16), vtbf[j])
            return m_new, l, acc

        m, l, acc = lax.fori_loop(0, qi, body, (m, l, acc))
        outs.append((in_head, acc / l))

    o = outs[0][1]
    for in_head, oh in outs[1:]:
        o = jnp.where(in_head, oh, o)
    o_ref[...] = o


def _moba_seq(q, kt, vt):
    N, HD = q.shape
    B, _, T = kt.shape
    assert N == B * T and T % MOBA_BLOCK == 0 and HD % LANES == 0
    nb = T // MOBA_BLOCK
    ng = HD // LANES
    return pl.pallas_call(
        functools.partial(_moba_seq_kernel, nb=nb, scale=HEAD_DIM ** -0.5),
        grid=(B, ng, nb),
        in_specs=[pl.BlockSpec((MOBA_BLOCK, LANES), lambda b, g, i: (b * nb + i, g)),
                  pl.BlockSpec((1, LANES, T), lambda b, g, i: (b, g, 0)),
                  pl.BlockSpec((1, LANES, T), lambda b, g, i: (b, g, 0))],
        out_specs=pl.BlockSpec((MOBA_BLOCK, LANES), lambda b, g, i: (b * nb + i, g)),
        out_shape=jax.ShapeDtypeStruct((N, HD), F32),
        scratch_shapes=[pltpu.VMEM((nb, LANES, MOBA_BLOCK), BF16),
                        pltpu.VMEM((nb, LANES, MOBA_BLOCK), BF16),
                        pltpu.VMEM((LANES, nb), F32)],
        compiler_params=pltpu.CompilerParams(
            dimension_semantics=("arbitrary", "arbitrary", "arbitrary"),
            vmem_limit_bytes=VMEM_LIMIT),
        name="moba_seq",
    )(q, kt, vt)


def _moba_step_kernel(pt_ref, qt_ref, knt_ref, vnt_ref, ck_hbm, cv_hbm, ot_ref,
                      kbuf, vbuf, s_buf, ksem, vsem, *, n_pages, page, nb, scale):
    b = pl.program_id(0)
    nseq = pl.num_programs(0)
    slot = b % 2
    HD = qt_ref.shape[0]

    def page_copy(src_hbm, dst, sem, seq, sl, j):
        return pltpu.make_async_copy(src_hbm.at[pt_ref[seq, j]], dst.at[sl, j], sem.at[sl])

    def start_fetch(seq, sl):
        for j in range(n_pages):
            page_copy(ck_hbm, kbuf, ksem, seq, sl, j).start()
        for j in range(n_pages):
            page_copy(cv_hbm, vbuf, vsem, seq, sl, j).start()

    @pl.when(b == 0)
    def _():
        start_fetch(b, slot)
        ot_ref[...] = jnp.zeros(ot_ref.shape, F32)

    @pl.when(b + 1 < nseq)
    def _():
        start_fetch(b + 1, 1 - slot)

    seq_lane = lax.broadcasted_iota(jnp.int32, qt_ref.shape, 1) == b
    column = lambda ref: jnp.sum(jnp.where(seq_lane, ref[...], 0.0), axis=1, keepdims=True)
    q_col = column(qt_ref) * scale
    kn_col = column(knt_ref)
    vn_col = column(vnt_ref)
    q_b = jnp.broadcast_to(q_col, (HD, page))

    def head_sums(x):
        return jnp.sum(x.reshape(N_HEADS, HEAD_DIM, x.shape[1]), axis=1)

    def head_bcast(x):
        return jnp.broadcast_to(x[:, None, :], (N_HEADS, HEAD_DIM, x.shape[1])).reshape(HD, x.shape[1])

    for j in range(n_pages):
        page_copy(ck_hbm, kbuf, ksem, b, slot, j).wait()
    for j in range(n_pages):
        s_buf[:, j * page:(j + 1) * page] = head_sums(kbuf[slot, j] * q_b)

    gate = _block_sums(s_buf[...], nb) * (1.0 / MOBA_BLOCK)
    sel = _topk_select(gate, jnp.ones(gate.shape, jnp.bool_))

    s_own = head_sums(q_col * kn_col)
    m = s_own
    for j in range(nb):
        sj = jnp.where(sel[:, j:j + 1], s_buf[:, j * MOBA_BLOCK:(j + 1) * MOBA_BLOCK], NEG)
        s_buf[:, j * MOBA_BLOCK:(j + 1) * MOBA_BLOCK] = sj
        m = jnp.maximum(m, jnp.max(sj, axis=1, keepdims=True))
    p_own = jnp.exp(s_own - m)
    p = jnp.exp(s_buf[...] - m)
    l = p_own + jnp.sum(p, axis=1, keepdims=True)

    for j in range(n_pages):
        page_copy(cv_hbm, vbuf, vsem, b, slot, j).wait()
    acc = jnp.zeros((HD, page), F32)
    for j in range(n_pages):
        acc = acc + vbuf[slot, j] * head_bcast(p[:, j * page:(j + 1) * page])
    o_col = jnp.sum(acc, axis=1, keepdims=True) + head_bcast(p_own) * vn_col
    o_col = o_col / head_bcast(l)
    ot_ref[...] = jnp.where(seq_lane, o_col, ot_ref[...])


def _moba_step(qt, knt, vnt, ck, cv, page_table):
    HD, N = qt.shape
    page = ck.shape[2]
    n_pages = page_table.shape[1]
    past_len = n_pages * page
    assert past_len % MOBA_BLOCK == 0 and MOBA_BLOCK % page == 0 and page % LANES == 0
    nb = past_len // MOBA_BLOCK
    assert nb >= 1
    full = pl.BlockSpec((HD, N), lambda b, pt: (0, 0))
    return pl.pallas_call(
        functools.partial(_moba_step_kernel, n_pages=n_pages, page=page, nb=nb,
                          scale=HEAD_DIM ** -0.5),
        grid_spec=pltpu.PrefetchScalarGridSpec(
            num_scalar_prefetch=1,
            grid=(N,),
            in_specs=[full, full, full,
                      pl.BlockSpec(memory_space=pl.ANY),
                      pl.BlockSpec(memory_space=pl.ANY)],
            out_specs=full,
            scratch_shapes=[pltpu.VMEM((2, n_pages, HD, page), F32),
                            pltpu.VMEM((2, n_pages, HD, page), F32),
                            pltpu.VMEM((N_HEADS, past_len), F32),
                            pltpu.SemaphoreType.DMA((2,)),
                            pltpu.SemaphoreType.DMA((2,))]),
        out_shape=jax.ShapeDtypeStruct((HD, N), F32),
        compiler_params=pltpu.CompilerParams(
            dimension_semantics=("arbitrary",), vmem_limit_bytes=VMEM_LIMIT),
        name="moba_step",
    )(page_table, qt, knt, vnt, ck, cv)


def kernel(x_prompt, x_sample, state_conv, state_h, cache_k, cache_v, page_table, norm_mix, norm_mlp, w_ff1, w_ff2, w_rg_in, b_rg_in, conv_w, conv_b, w_gate_a, b_gate_a, w_gate_i, b_gate_i, lru_lambda, w_rg_out, b_rg_out, norm_kv, w_kv, w_q, w_o, norm_out):
    B, T, D = x_prompt.shape
    NS, TS, _ = x_sample.shape
    depth = norm_mix.shape[0]
    assert depth == 2 and w_rg_in.shape[0] == 1 and w_q.shape[0] == 1 and TS == 1
    HD = N_HEADS * HEAD_DIM
    d_rnn = w_rg_in.shape[2] // 2
    hist = CONV_W - 1
    n_phys, page = cache_k.shape[0], cache_k.shape[1]
    bf = lambda w: w.astype(BF16)

    rg = (norm_mix[0], bf(w_rg_in[0]), b_rg_in[0], conv_w[0], conv_b[0],
          bf(w_gate_a[0]), b_gate_a[0].reshape(-1), bf(w_gate_i[0]), b_gate_i[0].reshape(-1),
          lru_lambda[0], bf(w_rg_out[0]), b_rg_out[0])
    mlp0 = (norm_mlp[0], bf(w_ff1[0]), bf(w_ff2[0]), norm_kv, bf(w_kv.T), norm_mix[1], bf(w_q[0]))
    mlp1 = (bf(w_o[0]), norm_mlp[1], bf(w_ff1[1]), bf(w_ff2[1]), norm_out)

    def heads_last(xt):
        lead = xt.shape[:-2]
        xt = xt.reshape(lead + (N_HEADS, HEAD_DIM, xt.shape[-1]))
        return jnp.moveaxis(xt, -1, -3)

    x1, conv_p, h_p = _rglru_seq(x_prompt, jnp.zeros((B, hist, d_rnn), F32),
                                 jnp.zeros((B, d_rnn), F32), *rg, tm=256)
    x2, kt_p, vt_p, q_p = _mlp_kv(x1, B, *mlp0, tm=256)
    attn_p = _moba_seq(q_p, kt_p, vt_p)
    y_p = _attn_out_mlp(x2, attn_p, *mlp1, tm=256)

    xs = x_sample.reshape(NS, D)
    x1s, conv_s, h_s = _rglru_step(xs, jnp.swapaxes(state_conv[0], 0, 1), state_h[0], *rg)
    x2s, kt_s, vt_s, q_s = _mlp_kv(x1s, 1, *mlp0, tm=NS)
    ck = jnp.transpose(cache_k, (0, 2, 3, 1)).reshape(n_phys, HD, page)
    cv = jnp.transpose(cache_v, (0, 2, 3, 1)).reshape(n_phys, HD, page)
    attn_s = _moba_step(q_s.T, kt_s[0], vt_s[0], ck, cv, page_table).T
    y_s = _attn_out_mlp(x2s, attn_s, *mlp1, tm=NS)

    return (y_p.reshape(B, T, D), y_s.reshape(NS, 1, D),
            conv_p[None], h_p[None], heads_last(kt_p), heads_last(vt_p),
            jnp.swapaxes(conv_s, 0, 1)[None], h_s[None],
            heads_last(kt_s[0])[:, None], heads_last(vt_s[0])[:, None])
```

```python
import functools
import math

import jax
import jax.numpy as jnp
from jax import lax
from jax.experimental import pallas as pl
from jax.experimental.pallas import tpu as pltpu

N_HEADS = 16
HEAD_DIM = 64
N_LRU_BLOCKS = 4
CONV_W = 4
LRU_C = 8.0
MOBA_BLOCK = 256
MOBA_TOPK = 3
EPS = 1e-6
NEG = -1e30
LOG2E = math.log2(math.e)

LANES = 128
SUBLANES = 8
HEADS_PER_GROUP = LANES // HEAD_DIM
VMEM_LIMIT = 56 * 1024 * 1024

BF16 = jnp.bfloat16
F32 = jnp.float32


def _rmsnorm(x, g):
    return x * lax.rsqrt(jnp.mean(x * x, axis=-1, keepdims=True) + EPS) * g


def _dot(a, b, precision=None):
    return jnp.dot(a, b, preferred_element_type=F32, precision=precision)


def _dot_nt(a, b):
    return lax.dot_general(a, b, (((1,), (1,)), ((), ())), preferred_element_type=F32)


def _gelu_tanh(x):
    c = math.sqrt(2.0 / math.pi)
    return x * (0.5 * (1.0 + jnp.tanh(c * (x + 0.044715 * (x * x * x)))))


def _softplus(y):
    return jnp.maximum(y, 0.0) + jnp.log1p(jnp.exp(-jnp.abs(y)))


def _neg_expm1(x):
    return jnp.tanh(-0.5 * x) * (jnp.exp(x) + 1.0)


def _lru_coeffs(c, wga_ref, bga, wgi_ref, bgi, lam):
    blk = c.shape[1] // N_LRU_BLOCKS
    cb = c.astype(BF16)
    ra, ri = [], []
    for n in range(N_LRU_BLOCKS):
        cn = cb[:, n * blk:(n + 1) * blk]
        ra.append(_dot(cn, wga_ref[n]))
        ri.append(_dot(cn, wgi_ref[n]))
    r = jax.nn.sigmoid(jnp.concatenate(ra, axis=1) + bga)
    i = jax.nn.sigmoid(jnp.concatenate(ri, axis=1) + bgi)
    log_a = -LRU_C * r * _softplus(-lam)
    a = jnp.exp(log_a)
    mult = jnp.sqrt(_neg_expm1(2.0 * log_a))
    return a, mult * (i * c)


def _rglru_seq_kernel(x_ref, conv0_ref, h0_ref, g_ref, win_ref, bin_ref, cw_ref, cb_ref,
                      wga_ref, bga_ref, wgi_ref, bgi_ref, lam_ref, wout_ref, bout_ref,
                      x1_ref, convn_ref, hn_ref,
                      ubuf, aloc, hloc, hcar, *, tm, d_rnn):
    t = pl.program_id(1)
    nt = pl.num_programs(1)
    hist = CONV_W - 1

    @pl.when(t == 0)
    def _():
        ubuf[SUBLANES - hist:SUBLANES, :] = conv0_ref[0]
        hcar[...] = h0_ref[0]

    x = x_ref[...]
    xn = _rmsnorm(x, g_ref[...]).astype(BF16)
    proj = _dot(xn, win_ref[...]) + bin_ref[...]
    gate = _gelu_tanh(proj[:, :d_rnn])
    u = proj[:, d_rnn:]
    ubuf[SUBLANES:SUBLANES + tm, :] = u

    c = cb_ref[...] + ubuf[SUBLANES - hist:SUBLANES - hist + tm, :] * cw_ref[0:1, :]
    for k in range(1, CONV_W):
        off = SUBLANES - hist + k
        c = c + ubuf[off:off + tm, :] * cw_ref[k:k + 1, :]

    a, b = _lru_coeffs(c, wga_ref, bga_ref[...], wgi_ref, bgi_ref[...], lam_ref[...])

    row = lax.broadcasted_iota(jnp.int32, a.shape, 0) % SUBLANES
    for s in (1, 2, 4):
        ok = row >= s
        a_sh = pltpu.roll(a, s, 0)
        b_sh = pltpu.roll(b, s, 0)
        b = jnp.where(ok, a * b_sh + b, b)
        a = jnp.where(ok, a * a_sh, a)
    aloc[...] = a
    hloc[...] = b

    def body(gi, carry):
        r0 = pl.multiple_of(gi * SUBLANES, SUBLANES)
        h = hloc[pl.ds(r0, SUBLANES), :] + aloc[pl.ds(r0, SUBLANES), :] * carry
        hloc[pl.ds(r0, SUBLANES), :] = h
        return h[SUBLANES - 1:SUBLANES, :]

    hcar[...] = lax.fori_loop(0, tm // SUBLANES, body, hcar[...])

    hg = (hloc[...] * gate).astype(BF16)
    x1_ref[...] = x + _dot(hg, wout_ref[...]) + bout_ref[...]

    ubuf[SUBLANES - hist:SUBLANES, :] = ubuf[SUBLANES + tm - hist:SUBLANES + tm, :]

    @pl.when(t == nt - 1)
    def _():
        convn_ref[0] = ubuf[SUBLANES - hist:SUBLANES, :]
        hn_ref[0] = hcar[...]


def _rglru_seq(x, conv0, h0, g, w_in, b_in, cw, cb, wga, bga, wgi, bgi, lam, w_out, b_out, *, tm):
    B, T, D = x.shape
    d_rnn = w_in.shape[1] // 2
    hist = CONV_W - 1
    assert T % tm == 0 and tm % SUBLANES == 0 and tm >= hist
    nt = T // tm
    x2 = x.reshape(B * T, D)
    full = lambda a: pl.BlockSpec(a.shape, lambda b, t: (0,) * a.ndim)
    row = lambda v: v.reshape(1, -1)
    args = (x2, conv0, h0.reshape(B, 1, d_rnn), row(g), w_in, row(b_in), cw, row(cb),
            wga, row(bga), wgi, row(bgi), row(lam), w_out, row(b_out))
    in_specs = [pl.BlockSpec((tm, D), lambda b, t: (b * nt + t, 0)),
                pl.BlockSpec((1, hist, d_rnn), lambda b, t: (b, 0, 0)),
                pl.BlockSpec((1, 1, d_rnn), lambda b, t: (b, 0, 0))]
    in_specs += [full(a) for a in args[3:]]
    x1, convn, hn = pl.pallas_call(
        functools.partial(_rglru_seq_kernel, tm=tm, d_rnn=d_rnn),
        grid=(B, nt),
        in_specs=in_specs,
        out_specs=[pl.BlockSpec((tm, D), lambda b, t: (b * nt + t, 0)),
                   pl.BlockSpec((1, hist, d_rnn), lambda b, t: (b, 0, 0)),
                   pl.BlockSpec((1, 1, d_rnn), lambda b, t: (b, 0, 0))],
        out_shape=[jax.ShapeDtypeStruct((B * T, D), F32),
                   jax.ShapeDtypeStruct((B, hist, d_rnn), F32),
                   jax.ShapeDtypeStruct((B, 1, d_rnn), F32)],
        scratch_shapes=[pltpu.VMEM((tm + SUBLANES, d_rnn), F32),
                        pltpu.VMEM((tm, d_rnn), F32),
                        pltpu.VMEM((tm, d_rnn), F32),
                        pltpu.VMEM((1, d_rnn), F32)],
        compiler_params=pltpu.CompilerParams(
            dimension_semantics=("arbitrary", "arbitrary"), vmem_limit_bytes=VMEM_LIMIT),
        name="rglru_seq",
    )(*args)
    return x1, convn, hn.reshape(B, d_rnn)


def _rglru_step_kernel(x_ref, conv_ref, h0_ref, g_ref, win_ref, bin_ref, cw_ref, cb_ref,
                       wga_ref, bga_ref, wgi_ref, bgi_ref, lam_ref, wout_ref, bout_ref,
                       x1_ref, convn_ref, hn_ref, *, d_rnn):
    hist = CONV_W - 1
    x = x_ref[...]
    xn = _rmsnorm(x, g_ref[...]).astype(BF16)
    proj = _dot(xn, win_ref[...]) + bin_ref[...]
    gate = _gelu_tanh(proj[:, :d_rnn])
    u = proj[:, d_rnn:]
    c = cb_ref[...] + conv_ref[0] * cw_ref[0:1, :]
    for k in range(1, hist):
        c = c + conv_ref[k] * cw_ref[k:k + 1, :]
    c = c + u * cw_ref[hist:hist + 1, :]
    a, b = _lru_coeffs(c, wga_ref, bga_ref[...], wgi_ref, bgi_ref[...], lam_ref[...])
    h = a * h0_ref[...] + b
    hn_ref[...] = h
    for k in range(hist - 1):
        convn_ref[k] = conv_ref[k + 1]
    convn_ref[hist - 1] = u
    x1_ref[...] = x + _dot((h * gate).astype(BF16), wout_ref[...]) + bout_ref[...]


def _rglru_step(x, conv0, h0, g, w_in, b_in, cw, cb, wga, bga, wgi, bgi, lam, w_out, b_out):
    N, D = x.shape
    d_rnn = w_in.shape[1] // 2
    hist = CONV_W - 1
    row = lambda v: v.reshape(1, -1)
    args = (x, conv0, h0, row(g), w_in, row(b_in), cw, row(cb),
            wga, row(bga), wgi, row(bgi), row(lam), w_out, row(b_out))
    return pl.pallas_call(
        functools.partial(_rglru_step_kernel, d_rnn=d_rnn),
        out_shape=[jax.ShapeDtypeStruct((N, D), F32),
                   jax.ShapeDtypeStruct((hist, N, d_rnn), F32),
                   jax.ShapeDtypeStruct((N, d_rnn), F32)],
        compiler_params=pltpu.CompilerParams(vmem_limit_bytes=VMEM_LIMIT),
        name="rglru_step",
    )(*args)


def _sqrelu_mlp(xn_bf, w1_ref, w2_ref, fc):
    d_ff = w1_ref.shape[1]
    acc = None
    for c0 in range(0, d_ff, fc):
        h = jnp.maximum(_dot(xn_bf, w1_ref[:, c0:c0 + fc]), 0.0)
        part = _dot((h * h).astype(BF16), w2_ref[c0:c0 + fc, :])
        acc = part if acc is None else acc + part
    return acc


def _mlp_kv_kernel(x_ref, gm_ref, w1_ref, w2_ref, gkv_ref, wkvt_ref, gq_ref, wq_ref,
                   x2_ref, kt_ref, vt_ref, q_ref, *, fc):
    x = x_ref[...]
    x2 = x + _sqrelu_mlp(_rmsnorm(x, gm_ref[...]).astype(BF16), w1_ref, w2_ref, fc)
    x2_ref[...] = x2
    hd = kt_ref.shape[1]
    kvt = _dot_nt(wkvt_ref[...], _rmsnorm(x2, gkv_ref[...]).astype(BF16))
    kt_ref[0] = kvt[:hd, :]
    vt_ref[0] = kvt[hd:, :]
    q_ref[...] = _dot(_rmsnorm(x2, gq_ref[...]).astype(BF16), wq_ref[...])


def _mlp_kv(x, B, gm, w1, w2, gkv, wkvt, gq, wq, *, tm, fc=512):
    N, D = x.shape
    hd = wq.shape[1]
    T = N // B
    assert N == B * T and T % tm == 0
    nt = T // tm
    row = lambda v: v.reshape(1, -1)
    args = (x, row(gm), w1, w2, row(gkv), wkvt, row(gq), wq)
    full = lambda a: pl.BlockSpec(a.shape, lambda i: (0,) * a.ndim)
    tile = lambda w: pl.BlockSpec((tm, w), lambda i: (i, 0))
    tile_t = pl.BlockSpec((1, hd, tm), lambda i: (i // nt, 0, i % nt))
    return pl.pallas_call(
        functools.partial(_mlp_kv_kernel, fc=fc),
        grid=(N // tm,),
        in_specs=[tile(D)] + [full(a) for a in args[1:]],
        out_specs=[tile(D), tile_t, tile_t, tile(hd)],
        out_shape=[jax.ShapeDtypeStruct((N, D), F32),
                   jax.ShapeDtypeStruct((B, hd, T), F32),
                   jax.ShapeDtypeStruct((B, hd, T), F32),
                   jax.ShapeDtypeStruct((N, hd), F32)],
        compiler_params=pltpu.CompilerParams(
            dimension_semantics=("arbitrary",), vmem_limit_bytes=VMEM_LIMIT),
        name="mlp_kv",
    )(*args)


def _attn_out_mlp_kernel(x_ref, a_ref, wo_ref, gm_ref, w1_ref, w2_ref, go_ref, y_ref, *, fc):
    x3 = x_ref[...] + _dot(a_ref[...].astype(BF16), wo_ref[...])
    x4 = x3 + _sqrelu_mlp(_rmsnorm(x3, gm_ref[...]).astype(BF16), w1_ref, w2_ref, fc)
    y_ref[...] = _rmsnorm(x4, go_ref[...])


def _attn_out_mlp(x, attn, wo, gm, w1, w2, go, *, tm, fc=512):
    N, D = x.shape
    assert N % tm == 0
    row = lambda v: v.reshape(1, -1)
    args = (x, attn, wo, row(gm), w1, w2, row(go))
    full = lambda a: pl.BlockSpec(a.shape, lambda i: (0,) * a.ndim)
    tile = lambda w: pl.BlockSpec((tm, w), lambda i: (i, 0))
    return pl.pallas_call(
        functools.partial(_attn_out_mlp_kernel, fc=fc),
        grid=(N // tm,),
        in_specs=[tile(D), tile(attn.shape[1])] + [full(a) for a in args[2:]],
        out_specs=tile(D),
        out_shape=jax.ShapeDtypeStruct((N, D), F32),
        compiler_params=pltpu.CompilerParams(
            dimension_semantics=("arbitrary",), vmem_limit_bytes=VMEM_LIMIT),
        name="attn_out_mlp",
    )(*args)


def _topk_select(gate, allowed, axis=1):
    nb = gate.shape[axis]
    blk = lax.broadcasted_iota(jnp.int32, gate.shape, axis)
    g = jnp.where(allowed, gate, NEG)
    sel = jnp.zeros(gate.shape, jnp.bool_)
    for _ in range(MOBA_TOPK):
        m = jnp.max(g, axis=axis, keepdims=True)
        first = jnp.min(jnp.where(g == m, blk, nb), axis=axis, keepdims=True)
        pick = blk == first
        sel = jnp.logical_or(sel, pick)
        g = jnp.where(pick, -jnp.inf, g)
    return jnp.logical_and(sel, allowed)


def _block_sums(x, nb):
    blk = lax.broadcasted_iota(jnp.int32, (x.shape[0], nb), 1)
    out = jnp.zeros((x.shape[0], nb), F32)
    for j in range(nb):
        sj = jnp.sum(x[:, j * MOBA_BLOCK:(j + 1) * MOBA_BLOCK], axis=1, keepdims=True)
        out = jnp.where(blk == j, sj, out)
    return out


def _moba_gate_kernel(q_ref, kt_ref, bias_ref, kw, *, nb, nbp, tq):
    t = pl.program_id(2)
    blk_rows = MOBA_BLOCK

    @pl.when(t == 0)
    def _():
        km_t = _block_sums(kt_ref[0], nb) * (1.0 / blk_rows)
        km_t = jnp.concatenate([km_t, jnp.zeros((LANES, LANES - nb), F32)], axis=1)
        km = km_t.T[0:nbp, :]
        feat = lax.broadcasted_iota(jnp.int32, km.shape, 1)
        for hh in range(HEADS_PER_GROUP):
            own = jnp.logical_and(feat >= hh * HEAD_DIM, feat < (hh + 1) * HEAD_DIM)
            k_h = jnp.where(own, km, 0.0)
            k_hi = k_h.astype(BF16)
            k_lo = (k_h - k_hi.astype(F32)).astype(BF16)
            kw[hh, 0:nbp, :] = jnp.concatenate([k_hi, k_hi], axis=1)
            kw[hh, nbp:, :] = jnp.concatenate([k_lo, jnp.zeros_like(k_lo)], axis=1)

    q2 = q_ref[...]
    q_hi = q2.astype(BF16)
    q_lo = (q2 - q_hi.astype(F32)).astype(BF16)
    q_cat = jnp.concatenate([q_hi, q_lo], axis=1)
    blk_t = lax.broadcasted_iota(jnp.int32, (nbp, tq), 0)
    q_blk = (t * tq + lax.broadcasted_iota(jnp.int32, (nbp, tq), 1)) // blk_rows
    for hh in range(HEADS_PER_GROUP):
        parts = _dot_nt(kw[hh], q_cat)
        gate_t = parts[0:nbp, :] + parts[nbp:, :]
        sel_t = _topk_select(gate_t, blk_t < q_blk, axis=0)
        bias_t = jnp.where(jnp.logical_or(sel_t, blk_t == q_blk), 0.0, NEG)
        bias_t = jnp.concatenate([bias_t, jnp.zeros((LANES - nbp, tq), F32)], axis=0)
        bias_ref[hh] = bias_t.T.astype(BF16)


def _moba_gate(q, kt, *, tq=1024):
    N, HD = q.shape
    B, _, T = kt.shape
    assert N == B * T and T % tq == 0 and tq % MOBA_BLOCK == 0 and HD % LANES == 0
    nb = T // MOBA_BLOCK
    nbp = -(-nb // (2 * SUBLANES)) * (2 * SUBLANES)
    assert nbp <= LANES
    nt = T // tq
    return pl.pallas_call(
        functools.partial(_moba_gate_kernel, nb=nb, nbp=nbp, tq=tq),
        grid=(B, HD // LANES, nt),
        in_specs=[pl.BlockSpec((tq, LANES), lambda b, g, t: (b * nt + t, g)),
                  pl.BlockSpec((1, LANES, T), lambda b, g, t: (b, g, 0))],
        out_specs=pl.BlockSpec((HEADS_PER_GROUP, tq, LANES), lambda b, g, t: (g, b * nt + t, 0)),
        out_shape=jax.ShapeDtypeStruct((HD // HEAD_DIM, N, LANES), BF16),
        scratch_shapes=[pltpu.VMEM((HEADS_PER_GROUP, 2 * nbp, 2 * LANES), BF16)],
        compiler_params=pltpu.CompilerParams(
            dimension_semantics=("arbitrary", "arbitrary", "arbitrary"),
            vmem_limit_bytes=VMEM_LIMIT),
        name="moba_gate",
    )(q, kt)


def _moba_seq_kernel(q_ref, bias_ref, kt_ref, vt_ref, o_ref, ktaug, vtaug, sbuf, *, bpc, scale):
    qi = pl.program_id(2)
    blk_rows = MOBA_BLOCK
    kc = bpc * blk_rows
    nchunk = ktaug.shape[0]
    n_lane_groups = kc // LANES

    @pl.when(qi == 0)
    def _():
        blk_row = lax.broadcasted_iota(jnp.int32, (LANES, kc), 0)
        blk_col = lax.broadcasted_iota(jnp.int32, (LANES, kc), 1) // blk_rows
        for c in range(nchunk):
            ktaug[c, 0:LANES, :] = kt_ref[0, :, c * kc:(c + 1) * kc].astype(BF16)
            ktaug[c, LANES:, :] = jnp.where(blk_row == blk_col + c * bpc, 1.0, 0.0).astype(BF16)
            vt_c = vt_ref[0, :, c * kc:(c + 1) * kc]
            for hh in range(HEADS_PER_GROUP):
                own = jnp.logical_and(blk_row >= hh * HEAD_DIM, blk_row < (hh + 1) * HEAD_DIM)
                ones_row = ((hh + 1) % HEADS_PER_GROUP) * HEAD_DIM
                vtaug[hh, c] = jnp.where(own, vt_c, jnp.where(blk_row == ones_row, 1.0, 0.0)).astype(BF16)

    q2 = q_ref[...]
    lane = lax.broadcasted_iota(jnp.int32, q2.shape, 1)
    c_own = qi // bpc
    r_id = lax.broadcasted_iota(jnp.int32, (blk_rows, kc), 0)
    c_id = lax.broadcasted_iota(jnp.int32, (blk_rows, kc), 1)
    own_off = (qi - c_own * bpc) * blk_rows
    keep_own = jnp.logical_or(c_id - own_off <= r_id,
                              jnp.logical_or(c_id < own_off, c_id >= own_off + blk_rows))

    def lane_group_max(s, m):
        for g in range(n_lane_groups):
            sg = s[:, g * LANES:(g + 1) * LANES]
            m = sg if m is None else jnp.maximum(m, sg)
        return m

    in_heads, q_augs, mruns = [], [], []
    for hh in range(HEADS_PER_GROUP):
        in_head = jnp.logical_and(lane >= hh * HEAD_DIM, lane < (hh + 1) * HEAD_DIM)
        qh = jnp.where(in_head, q2, 0.0)
        q_aug = jnp.concatenate([(qh * (scale * LOG2E)).astype(BF16), bias_ref[hh]], axis=1)
        s = jnp.where(keep_own, _dot(q_aug, ktaug[c_own]), NEG)
        sbuf[hh, c_own] = s
        in_heads.append(in_head)
        q_augs.append(q_aug)
        mruns.append(lane_group_max(s, None))

    def score_body(c, mruns):
        out = []
        for hh in range(HEADS_PER_GROUP):
            s = _dot(q_augs[hh], ktaug[c])
            sbuf[hh, c] = s
            out.append(lane_group_max(s, mruns[hh]))
        return tuple(out)

    mruns = lax.fori_loop(0, c_own, score_body, tuple(mruns))
    ms = [jnp.max(m, axis=1, keepdims=True) for m in mruns]

    def pv_body(c, accs):
        out = []
        for hh in range(HEADS_PER_GROUP):
            p = jnp.exp2(sbuf[hh, c] - ms[hh]).astype(BF16)
            out.append(accs[hh] + _dot_nt(p, vtaug[hh, c]))
        return tuple(out)

    zero = jnp.zeros((blk_rows, LANES), F32)
    accs = lax.fori_loop(0, c_own + 1, pv_body, (zero,) * HEADS_PER_GROUP)

    o = None
    for hh in range(HEADS_PER_GROUP):
        ones_lane = ((hh + 1) % HEADS_PER_GROUP) * HEAD_DIM
        oh = accs[hh] / accs[hh][:, ones_lane:ones_lane + 1]
        o = oh if o is None else jnp.where(in_heads[hh], oh, o)
    o_ref[...] = o


def _moba_seq(q, bias, kt, vt, *, bpc=4):
    N, HD = q.shape
    B, _, T = kt.shape
    assert N == B * T and T % (bpc * MOBA_BLOCK) == 0 and HD % LANES == 0
    assert HEADS_PER_GROUP >= 2
    nb = T // MOBA_BLOCK
    nchunk = nb // bpc
    kc = bpc * MOBA_BLOCK
    return pl.pallas_call(
        functools.partial(_moba_seq_kernel, bpc=bpc, scale=HEAD_DIM ** -0.5),
        grid=(B, HD // LANES, nb),
        in_specs=[pl.BlockSpec((MOBA_BLOCK, LANES), lambda b, g, i: (b * nb + i, g)),
                  pl.BlockSpec((HEADS_PER_GROUP, MOBA_BLOCK, LANES), lambda b, g, i: (g, b * nb + i, 0)),
                  pl.BlockSpec((1, LANES, T), lambda b, g, i: (b, g, 0)),
                  pl.BlockSpec((1, LANES, T), lambda b, g, i: (b, g, 0))],
        out_specs=pl.BlockSpec((MOBA_BLOCK, LANES), lambda b, g, i: (b * nb + i, g)),
        out_shape=jax.ShapeDtypeStruct((N, HD), F32),
        scratch_shapes=[pltpu.VMEM((nchunk, 2 * LANES, kc), BF16),
                        pltpu.VMEM((HEADS_PER_GROUP, nchunk, LANES, kc), BF16),
                        pltpu.VMEM((HEADS_PER_GROUP, nchunk, MOBA_BLOCK, kc), F32)],
        compiler_params=pltpu.CompilerParams(
            dimension_semantics=("arbitrary", "arbitrary", "arbitrary"),
            vmem_limit_bytes=VMEM_LIMIT),
        name="moba_seq",
    )(q, bias, kt, vt)


def _moba_step_kernel(pt_ref, qt_ref, knt_ref, vnt_ref, ck_hbm, cv_hbm, ot_ref,
                      kbuf, vbuf, s_buf, ksem, vsem, *, n_pages, page, nb, scale):
    b = pl.program_id(0)
    nseq = pl.num_programs(0)
    slot = b % 2
    HD = qt_ref.shape[0]

    def page_copy(src_hbm, dst, sem, seq, sl, j):
        return pltpu.make_async_copy(src_hbm.at[pt_ref[seq, j]], dst.at[sl, j], sem.at[sl])

    def start_fetch(seq, sl):
        for j in range(n_pages):
            page_copy(ck_hbm, kbuf, ksem, seq, sl, j).start()
        for j in range(n_pages):
            page_copy(cv_hbm, vbuf, vsem, seq, sl, j).start()

    @pl.when(b == 0)
    def _():
        start_fetch(b, slot)
        ot_ref[...] = jnp.zeros(ot_ref.shape, F32)

    @pl.when(b + 1 < nseq)
    def _():
        start_fetch(b + 1, 1 - slot)

    seq_lane = lax.broadcasted_iota(jnp.int32, qt_ref.shape, 1) == b
    column = lambda ref: jnp.sum(jnp.where(seq_lane, ref[...], 0.0), axis=1, keepdims=True)
    q_col = column(qt_ref) * scale
    kn_col = column(knt_ref)
    vn_col = column(vnt_ref)
    q_b = jnp.broadcast_to(q_col, (HD, page))

    def head_sums(x):
        return jnp.sum(x.reshape(N_HEADS, HEAD_DIM, x.shape[1]), axis=1)

    def head_bcast(x):
        return jnp.broadcast_to(x[:, None, :], (N_HEADS, HEAD_DIM, x.shape[1])).reshape(HD, x.shape[1])

    for j in range(n_pages):
        page_copy(ck_hbm, kbuf, ksem, b, slot, j).wait()
    for j in range(n_pages):
        s_buf[:, j * page:(j + 1) * page] = head_sums(kbuf[slot, j] * q_b)

    gate = _block_sums(s_buf[...], nb) * (1.0 / MOBA_BLOCK)
    sel = _topk_select(gate, jnp.ones(gate.shape, jnp.bool_))

    s_own = head_sums(q_col * kn_col)
    m = s_own
    for j in range(nb):
        sj = jnp.where(sel[:, j:j + 1], s_buf[:, j * MOBA_BLOCK:(j + 1) * MOBA_BLOCK], NEG)
        s_buf[:, j * MOBA_BLOCK:(j + 1) * MOBA_BLOCK] = sj
        m = jnp.maximum(m, jnp.max(sj, axis=1, keepdims=True))
    p_own = jnp.exp(s_own - m)
    p = jnp.exp(s_buf[...] - m)
    l = p_own + jnp.sum(p, axis=1, keepdims=True)

    for j in range(n_pages):
        page_copy(cv_hbm, vbuf, vsem, b, slot, j).wait()
    acc = jnp.zeros((HD, page), F32)
    for j in range(n_pages):
        acc = acc + vbuf[slot, j] * head_bcast(p[:, j * page:(j + 1) * page])
    o_col = jnp.sum(acc, axis=1, keepdims=True) + head_bcast(p_own) * vn_col
    o_col = o_col / head_bcast(l)
    ot_ref[...] = jnp.where(seq_lane, o_col, ot_ref[...])


def _moba_step(qt, knt, vnt, ck, cv, page_table):
    HD, N = qt.shape
    page = ck.shape[2]
    n_pages = page_table.shape[1]
    past_len = n_pages * page
    assert past_len % MOBA_BLOCK == 0 and MOBA_BLOCK % page == 0 and page % LANES == 0
    nb = past_len // MOBA_BLOCK
    assert nb >= 1
    full = pl.BlockSpec((HD, N), lambda b, pt: (0, 0))
    return pl.pallas_call(
        functools.partial(_moba_step_kernel, n_pages=n_pages, page=page, nb=nb,
                          scale=HEAD_DIM ** -0.5),
        grid_spec=pltpu.PrefetchScalarGridSpec(
            num_scalar_prefetch=1,
            grid=(N,),
            in_specs=[full, full, full,
                      pl.BlockSpec(memory_space=pl.ANY),
                      pl.BlockSpec(memory_space=pl.ANY)],
            out_specs=full,
            scratch_shapes=[pltpu.VMEM((2, n_pages, HD, page), F32),
                            pltpu.VMEM((2, n_pages, HD, page), F32),
                            pltpu.VMEM((N_HEADS, past_len), F32),
                            pltpu.SemaphoreType.DMA((2,)),
                            pltpu.SemaphoreType.DMA((2,))]),
        out_shape=jax.ShapeDtypeStruct((HD, N), F32),
        compiler_params=pltpu.CompilerParams(
            dimension_semantics=("arbitrary",), vmem_limit_bytes=VMEM_LIMIT),
        name="moba_step",
    )(page_table, qt, knt, vnt, ck, cv)


def kernel(x_prompt, x_sample, state_conv, state_h, cache_k, cache_v, page_table, norm_mix, norm_mlp, w_ff1, w_ff2, w_rg_in, b_rg_in, conv_w, conv_b, w_gate_a, b_gate_a, w_gate_i, b_gate_i, lru_lambda, w_rg_out, b_rg_out, norm_kv, w_kv, w_q, w_o, norm_out):
    B, T, D = x_prompt.shape
    NS, TS, _ = x_sample.shape
    depth = norm_mix.shape[0]
    assert depth == 2 and w_rg_in.shape[0] == 1 and w_q.shape[0] == 1 and TS == 1
    HD = N_HEADS * HEAD_DIM
    d_rnn = w_rg_in.shape[2] // 2
    hist = CONV_W - 1
    n_phys, page = cache_k.shape[0], cache_k.shape[1]
    bf = lambda w: w.astype(BF16)

    rg = (norm_mix[0], bf(w_rg_in[0]), b_rg_in[0], conv_w[0], conv_b[0],
          bf(w_gate_a[0]), b_gate_a[0].reshape(-1), bf(w_gate_i[0]), b_gate_i[0].reshape(-1),
          lru_lambda[0], bf(w_rg_out[0]), b_rg_out[0])
    mlp0 = (norm_mlp[0], bf(w_ff1[0]), bf(w_ff2[0]), norm_kv, bf(w_kv.T), norm_mix[1], bf(w_q[0]))
    mlp1 = (bf(w_o[0]), norm_mlp[1], bf(w_ff1[1]), bf(w_ff2[1]), norm_out)

    def heads_last(xt):
        lead = xt.shape[:-2]
        xt = xt.reshape(lead + (N_HEADS, HEAD_DIM, xt.shape[-1]))
        return jnp.moveaxis(xt, -1, -3)

    x1, conv_p, h_p = _rglru_seq(x_prompt, jnp.zeros((B, hist, d_rnn), F32),
                                 jnp.zeros((B, d_rnn), F32), *rg, tm=256)
    x2, kt_p, vt_p, q_p = _mlp_kv(x1, B, *mlp0, tm=256)
    attn_p = _moba_seq(q_p, _moba_gate(q_p, kt_p, tq=min(T, 1024)), kt_p, vt_p)
    y_p = _attn_out_mlp(x2, attn_p, *mlp1, tm=256)

    xs = x_sample.reshape(NS, D)
    x1s, conv_s, h_s = _rglru_step(xs, jnp.swapaxes(state_conv[0], 0, 1), state_h[0], *rg)
    x2s, kt_s, vt_s, q_s = _mlp_kv(x1s, 1, *mlp0, tm=NS)
    ck = jnp.transpose(cache_k, (0, 2, 3, 1)).reshape(n_phys, HD, page)
    cv = jnp.transpose(cache_v, (0, 2, 3, 1)).reshape(n_phys, HD, page)
    attn_s = _moba_step(q_s.T, kt_s[0], vt_s[0], ck, cv, page_table).T
    y_s = _attn_out_mlp(x2s, attn_s, *mlp1, tm=NS)

    return (y_p.reshape(B, T, D), y_s.reshape(NS, 1, D),
            conv_p[None], h_p[None], heads_last(kt_p), heads_last(vt_p),
            jnp.swapaxes(conv_s, 0, 1)[None], h_s[None],
            heads_last(kt_s[0])[:, None], heads_last(vt_s[0])[:, None])
```

```python
import functools
import math

import jax
import jax.numpy as jnp
from jax import lax
from jax.experimental import pallas as pl
from jax.experimental.pallas import tpu as pltpu

N_HEADS = 16
HEAD_DIM = 64
N_LRU_BLOCKS = 4
CONV_W = 4
LRU_C = 8.0
MOBA_BLOCK = 256
MOBA_TOPK = 3
EPS = 1e-6
NEG = -1e30
LOG2E = math.log2(math.e)

LANES = 128
SUBLANES = 8
HEADS_PER_GROUP = LANES // HEAD_DIM
VMEM_LIMIT = 56 * 1024 * 1024

BF16 = jnp.bfloat16
F32 = jnp.float32


def _rmsnorm(x, g):
    return x * lax.rsqrt(jnp.mean(x * x, axis=-1, keepdims=True) + EPS) * g


def _rmsnorm_mxu(x, g):
    d = x.shape[1]
    ss = jnp.dot((x * x).astype(BF16), jnp.ones((d, LANES), BF16), preferred_element_type=F32)
    inv = lax.rsqrt(ss * (1.0 / d) + EPS)
    return x * jnp.concatenate([inv] * (d // LANES), axis=1) * g


def _dot(a, b, precision=None):
    return jnp.dot(a, b, preferred_element_type=F32, precision=precision)


def _dot_nt(a, b):
    return lax.dot_general(a, b, (((1,), (1,)), ((), ())), preferred_element_type=F32)


def _gelu_tanh(x):
    c = math.sqrt(2.0 / math.pi)
    return x * (0.5 * (1.0 + jnp.tanh(c * (x + 0.044715 * (x * x * x)))))


def _softplus(y):
    return jnp.maximum(y, 0.0) + jnp.log1p(jnp.exp(-jnp.abs(y)))


def _neg_expm1(x):
    return jnp.tanh(-0.5 * x) * (jnp.exp(x) + 1.0)


def _lru_coeffs(c, wga_ref, bga, wgi_ref, bgi, lam):
    blk = c.shape[1] // N_LRU_BLOCKS
    cb = c.astype(BF16)
    ra, ri = [], []
    for n in range(N_LRU_BLOCKS):
        cn = cb[:, n * blk:(n + 1) * blk]
        ra.append(_dot(cn, wga_ref[n]))
        ri.append(_dot(cn, wgi_ref[n]))
    r = jax.nn.sigmoid(jnp.concatenate(ra, axis=1) + bga)
    i = jax.nn.sigmoid(jnp.concatenate(ri, axis=1) + bgi)
    log_a = -LRU_C * r * _softplus(-lam)
    a = jnp.exp(log_a)
    mult = jnp.sqrt(_neg_expm1(2.0 * log_a))
    return a, mult * (i * c)


def _rglru_seq_kernel(x_ref, conv0_ref, h0_ref, g_ref, win_ref, bin_ref, cw_ref, cb_ref,
                      wga_ref, bga_ref, wgi_ref, bgi_ref, lam_ref, wout_ref, bout_ref,
                      x1_ref, convn_ref, hn_ref,
                      ubuf, aloc, hloc, hcar, *, tm, d_rnn):
    t = pl.program_id(1)
    nt = pl.num_programs(1)
    hist = CONV_W - 1

    @pl.when(t == 0)
    def _():
        ubuf[SUBLANES - hist:SUBLANES, :] = conv0_ref[0]
        hcar[...] = h0_ref[0]

    x = x_ref[...]
    xn = _rmsnorm_mxu(x, g_ref[...]).astype(BF16)
    proj = _dot(xn, win_ref[...]) + bin_ref[...]
    gate = _gelu_tanh(proj[:, :d_rnn])
    u = proj[:, d_rnn:]
    ubuf[SUBLANES:SUBLANES + tm, :] = u

    c = cb_ref[...] + ubuf[SUBLANES - hist:SUBLANES - hist + tm, :] * cw_ref[0:1, :]
    for k in range(1, CONV_W):
        off = SUBLANES - hist + k
        c = c + ubuf[off:off + tm, :] * cw_ref[k:k + 1, :]

    a, b = _lru_coeffs(c, wga_ref, bga_ref[...], wgi_ref, bgi_ref[...], lam_ref[...])

    a = a.reshape(tm // SUBLANES, SUBLANES, d_rnn)
    b = b.reshape(tm // SUBLANES, SUBLANES, d_rnn)
    row = lax.broadcasted_iota(jnp.int32, a.shape, 1)
    for s in (1, 2, 4):
        ok = row >= s
        a_sh = pltpu.roll(a, s, 1)
        b_sh = pltpu.roll(b, s, 1)
        b = jnp.where(ok, a * b_sh + b, b)
        a = jnp.where(ok, a * a_sh, a)
    aloc[...] = a.reshape(tm, d_rnn)
    hloc[...] = b.reshape(tm, d_rnn)

    def body(gi, carry):
        r0 = pl.multiple_of(gi * SUBLANES, SUBLANES)
        h = hloc[pl.ds(r0, SUBLANES), :] + aloc[pl.ds(r0, SUBLANES), :] * carry
        hloc[pl.ds(r0, SUBLANES), :] = h
        return h[SUBLANES - 1:SUBLANES, :]

    hcar[...] = lax.fori_loop(0, tm // SUBLANES, body, hcar[...])

    hg = (hloc[...] * gate).astype(BF16)
    x1_ref[...] = x + _dot(hg, wout_ref[...]) + bout_ref[...]

    ubuf[SUBLANES - hist:SUBLANES, :] = ubuf[SUBLANES + tm - hist:SUBLANES + tm, :]

    @pl.when(t == nt - 1)
    def _():
        convn_ref[0] = ubuf[SUBLANES - hist:SUBLANES, :]
        hn_ref[0] = hcar[...]


def _rglru_seq(x, conv0, h0, g, w_in, b_in, cw, cb, wga, bga, wgi, bgi, lam, w_out, b_out, *, tm):
    B, T, D = x.shape
    d_rnn = w_in.shape[1] // 2
    hist = CONV_W - 1
    assert T % tm == 0 and tm % SUBLANES == 0 and tm >= hist
    nt = T // tm
    x2 = x.reshape(B * T, D)
    full = lambda a: pl.BlockSpec(a.shape, lambda b, t: (0,) * a.ndim)
    row = lambda v: v.reshape(1, -1)
    args = (x2, conv0, h0.reshape(B, 1, d_rnn), row(g), w_in, row(b_in), cw, row(cb),
            wga, row(bga), wgi, row(bgi), row(lam), w_out, row(b_out))
    in_specs = [pl.BlockSpec((tm, D), lambda b, t: (b * nt + t, 0)),
                pl.BlockSpec((1, hist, d_rnn), lambda b, t: (b, 0, 0)),
                pl.BlockSpec((1, 1, d_rnn), lambda b, t: (b, 0, 0))]
    in_specs += [full(a) for a in args[3:]]
    x1, convn, hn = pl.pallas_call(
        functools.partial(_rglru_seq_kernel, tm=tm, d_rnn=d_rnn),
        grid=(B, nt),
        in_specs=in_specs,
        out_specs=[pl.BlockSpec((tm, D), lambda b, t: (b * nt + t, 0)),
                   pl.BlockSpec((1, hist, d_rnn), lambda b, t: (b, 0, 0)),
                   pl.BlockSpec((1, 1, d_rnn), lambda b, t: (b, 0, 0))],
        out_shape=[jax.ShapeDtypeStruct((B * T, D), F32),
                   jax.ShapeDtypeStruct((B, hist, d_rnn), F32),
                   jax.ShapeDtypeStruct((B, 1, d_rnn), F32)],
        scratch_shapes=[pltpu.VMEM((tm + SUBLANES, d_rnn), F32),
                        pltpu.VMEM((tm, d_rnn), F32),
                        pltpu.VMEM((tm, d_rnn), F32),
                        pltpu.VMEM((1, d_rnn), F32)],
        compiler_params=pltpu.CompilerParams(
            dimension_semantics=("arbitrary", "arbitrary"), vmem_limit_bytes=VMEM_LIMIT),
        name="rglru_seq",
    )(*args)
    return x1, convn, hn.reshape(B, d_rnn)


def _rglru_step_kernel(x_ref, conv_ref, h0_ref, g_ref, win_ref, bin_ref, cw_ref, cb_ref,
                       wga_ref, bga_ref, wgi_ref, bgi_ref, lam_ref, wout_ref, bout_ref,
                       x1_ref, convn_ref, hn_ref, *, d_rnn):
    hist = CONV_W - 1
    x = x_ref[...]
    xn = _rmsnorm(x, g_ref[...]).astype(BF16)
    proj = _dot(xn, win_ref[...]) + bin_ref[...]
    gate = _gelu_tanh(proj[:, :d_rnn])
    u = proj[:, d_rnn:]
    c = cb_ref[...] + conv_ref[0] * cw_ref[0:1, :]
    for k in range(1, hist):
        c = c + conv_ref[k] * cw_ref[k:k + 1, :]
    c = c + u * cw_ref[hist:hist + 1, :]
    a, b = _lru_coeffs(c, wga_ref, bga_ref[...], wgi_ref, bgi_ref[...], lam_ref[...])
    h = a * h0_ref[...] + b
    hn_ref[...] = h
    for k in range(hist - 1):
        convn_ref[k] = conv_ref[k + 1]
    convn_ref[hist - 1] = u
    x1_ref[...] = x + _dot((h * gate).astype(BF16), wout_ref[...]) + bout_ref[...]


def _rglru_step(x, conv0, h0, g, w_in, b_in, cw, cb, wga, bga, wgi, bgi, lam, w_out, b_out):
    N, D = x.shape
    d_rnn = w_in.shape[1] // 2
    hist = CONV_W - 1
    row = lambda v: v.reshape(1, -1)
    args = (x, conv0, h0, row(g), w_in, row(b_in), cw, row(cb),
            wga, row(bga), wgi, row(bgi), row(lam), w_out, row(b_out))
    return pl.pallas_call(
        functools.partial(_rglru_step_kernel, d_rnn=d_rnn),
        out_shape=[jax.ShapeDtypeStruct((N, D), F32),
                   jax.ShapeDtypeStruct((hist, N, d_rnn), F32),
                   jax.ShapeDtypeStruct((N, d_rnn), F32)],
        compiler_params=pltpu.CompilerParams(vmem_limit_bytes=VMEM_LIMIT),
        name="rglru_step",
    )(*args)


def _sqrelu_mlp(xn_bf, w1_ref, w2_ref, fc):
    d_ff = w1_ref.shape[1]
    acc = None
    for c0 in range(0, d_ff, fc):
        h = jnp.maximum(_dot(xn_bf, w1_ref[:, c0:c0 + fc]), 0.0)
        part = _dot((h * h).astype(BF16), w2_ref[c0:c0 + fc, :])
        acc = part if acc is None else acc + part
    return acc


def _mlp_kv_kernel(x_ref, gm_ref, w1_ref, w2_ref, gkv_ref, wkvt_ref, gq_ref, wq_ref,
                   x2_ref, kt_ref, vt_ref, q_ref, *, fc):
    x = x_ref[...]
    x2 = x + _sqrelu_mlp(_rmsnorm(x, gm_ref[...]).astype(BF16), w1_ref, w2_ref, fc)
    x2_ref[...] = x2
    hd = kt_ref.shape[1]
    kvt = _dot_nt(wkvt_ref[...], _rmsnorm(x2, gkv_ref[...]).astype(BF16))
    kt_ref[0] = kvt[:hd, :]
    vt_ref[0] = kvt[hd:, :]
    q_ref[...] = _dot(_rmsnorm(x2, gq_ref[...]).astype(BF16), wq_ref[...])


def _mlp_kv(x, B, gm, w1, w2, gkv, wkvt, gq, wq, *, tm, fc=512):
    N, D = x.shape
    hd = wq.shape[1]
    T = N // B
    assert N == B * T and T % tm == 0
    nt = T // tm
    row = lambda v: v.reshape(1, -1)
    args = (x, row(gm), w1, w2, row(gkv), wkvt, row(gq), wq)
    full = lambda a: pl.BlockSpec(a.shape, lambda i: (0,) * a.ndim)
    tile = lambda w: pl.BlockSpec((tm, w), lambda i: (i, 0))
    tile_t = pl.BlockSpec((1, hd, tm), lambda i: (i // nt, 0, i % nt))
    return pl.pallas_call(
        functools.partial(_mlp_kv_kernel, fc=fc),
        grid=(N // tm,),
        in_specs=[tile(D)] + [full(a) for a in args[1:]],
        out_specs=[tile(D), tile_t, tile_t, tile(hd)],
        out_shape=[jax.ShapeDtypeStruct((N, D), F32),
                   jax.ShapeDtypeStruct((B, hd, T), F32),
                   jax.ShapeDtypeStruct((B, hd, T), F32),
                   jax.ShapeDtypeStruct((N, hd), F32)],
        compiler_params=pltpu.CompilerParams(
            dimension_semantics=("arbitrary",), vmem_limit_bytes=VMEM_LIMIT),
        name="mlp_kv",
    )(*args)


def _attn_out_mlp_kernel(x_ref, a_ref, wo_ref, gm_ref, w1_ref, w2_ref, go_ref, y_ref, *, fc):
    x3 = x_ref[...] + _dot(a_ref[...].astype(BF16), wo_ref[...])
    x4 = x3 + _sqrelu_mlp(_rmsnorm(x3, gm_ref[...]).astype(BF16), w1_ref, w2_ref, fc)
    y_ref[...] = _rmsnorm(x4, go_ref[...])


def _attn_out_mlp(x, attn, wo, gm, w1, w2, go, *, tm, fc=512):
    N, D = x.shape
    assert N % tm == 0
    row = lambda v: v.reshape(1, -1)
    args = (x, attn, wo, row(gm), w1, w2, row(go))
    full = lambda a: pl.BlockSpec(a.shape, lambda i: (0,) * a.ndim)
    tile = lambda w: pl.BlockSpec((tm, w), lambda i: (i, 0))
    return pl.pallas_call(
        functools.partial(_attn_out_mlp_kernel, fc=fc),
        grid=(N // tm,),
        in_specs=[tile(D), tile(attn.shape[1])] + [full(a) for a in args[2:]],
        out_specs=tile(D),
        out_shape=jax.ShapeDtypeStruct((N, D), F32),
        compiler_params=pltpu.CompilerParams(
            dimension_semantics=("arbitrary",), vmem_limit_bytes=VMEM_LIMIT),
        name="attn_out_mlp",
    )(*args)


def _topk_select(gate, allowed, axis=1):
    nb = gate.shape[axis]
    blk = lax.broadcasted_iota(jnp.int32, gate.shape, axis)
    g = jnp.where(allowed, gate, NEG)
    sel = jnp.zeros(gate.shape, jnp.bool_)
    for _ in range(MOBA_TOPK):
        m = jnp.max(g, axis=axis, keepdims=True)
        first = jnp.min(jnp.where(g == m, blk, nb), axis=axis, keepdims=True)
        pick = blk == first
        sel = jnp.logical_or(sel, pick)
        g = jnp.where(pick, -jnp.inf, g)
    return jnp.logical_and(sel, allowed)


def _block_sums(x, nb):
    blk = lax.broadcasted_iota(jnp.int32, (x.shape[0], nb), 1)
    out = jnp.zeros((x.shape[0], nb), F32)
    for j in range(nb):
        sj = jnp.sum(x[:, j * MOBA_BLOCK:(j + 1) * MOBA_BLOCK], axis=1, keepdims=True)
        out = jnp.where(blk == j, sj, out)
    return out


def _moba_gate_kernel(q_ref, kt_ref, bias_ref, kw, *, nb, nbp, tq):
    t = pl.program_id(2)
    blk_rows = MOBA_BLOCK

    @pl.when(t == 0)
    def _():
        km_t = _block_sums(kt_ref[0], nb) * (1.0 / blk_rows)
        km_t = jnp.concatenate([km_t, jnp.zeros((LANES, LANES - nb), F32)], axis=1)
        km = km_t.T[0:nbp, :]
        feat = lax.broadcasted_iota(jnp.int32, km.shape, 1)
        for hh in range(HEADS_PER_GROUP):
            own = jnp.logical_and(feat >= hh * HEAD_DIM, feat < (hh + 1) * HEAD_DIM)
            k_h = jnp.where(own, km, 0.0)
            k_hi = k_h.astype(BF16)
            k_lo = (k_h - k_hi.astype(F32)).astype(BF16)
            kw[hh, 0:nbp, :] = jnp.concatenate([k_hi, k_hi], axis=1)
            kw[hh, nbp:, :] = jnp.concatenate([k_lo, jnp.zeros_like(k_lo)], axis=1)

    q2 = q_ref[...]
    q_hi = q2.astype(BF16)
    q_lo = (q2 - q_hi.astype(F32)).astype(BF16)
    q_cat = jnp.concatenate([q_hi, q_lo], axis=1)
    blk_t = lax.broadcasted_iota(jnp.int32, (nbp, tq), 0)
    q_blk = (t * tq + lax.broadcasted_iota(jnp.int32, (nbp, tq), 1)) // blk_rows
    for hh in range(HEADS_PER_GROUP):
        parts = _dot_nt(kw[hh], q_cat)
        gate_t = parts[0:nbp, :] + parts[nbp:, :]
        sel_t = _topk_select(gate_t, blk_t < q_blk, axis=0)
        bias_t = jnp.where(jnp.logical_or(sel_t, blk_t == q_blk), 0.0, NEG)
        bias_t = jnp.concatenate([bias_t, jnp.zeros((LANES - nbp, tq), F32)], axis=0)
        bias_ref[hh] = bias_t.T.astype(BF16)


def _moba_gate(q, kt, *, tq=1024):
    N, HD = q.shape
    B, _, T = kt.shape
    assert N == B * T and T % tq == 0 and tq % MOBA_BLOCK == 0 and HD % LANES == 0
    nb = T // MOBA_BLOCK
    nbp = -(-nb // (2 * SUBLANES)) * (2 * SUBLANES)
    assert nbp <= LANES
    nt = T // tq
    return pl.pallas_call(
        functools.partial(_moba_gate_kernel, nb=nb, nbp=nbp, tq=tq),
        grid=(B, HD // LANES, nt),
        in_specs=[pl.BlockSpec((tq, LANES), lambda b, g, t: (b * nt + t, g)),
                  pl.BlockSpec((1, LANES, T), lambda b, g, t: (b, g, 0))],
        out_specs=pl.BlockSpec((HEADS_PER_GROUP, tq, LANES), lambda b, g, t: (g, b * nt + t, 0)),
        out_shape=jax.ShapeDtypeStruct((HD // HEAD_DIM, N, LANES), BF16),
        scratch_shapes=[pltpu.VMEM((HEADS_PER_GROUP, 2 * nbp, 2 * LANES), BF16)],
        compiler_params=pltpu.CompilerParams(
            dimension_semantics=("arbitrary", "arbitrary", "arbitrary"),
            vmem_limit_bytes=VMEM_LIMIT),
        name="moba_gate",
    )(q, kt)


def _moba_seq_kernel(q_ref, bias_ref, kt_ref, vt_ref, o_ref, ktaug, vtaug, sbuf, *, bpc, scale):
    qi = pl.program_id(2)
    blk_rows = MOBA_BLOCK
    kc = bpc * blk_rows
    nchunk = ktaug.shape[0]

    @pl.when(qi == 0)
    def _():
        key_blk = lax.broadcasted_iota(jnp.int32, (kc, LANES), 0) // blk_rows
        blk_lane = lax.broadcasted_iota(jnp.int32, (kc, LANES), 1)
        feat_row = lax.broadcasted_iota(jnp.int32, (LANES, kc), 0)
        for c in range(nchunk):
            ktaug[c, :, 0:LANES] = kt_ref[0, :, c * kc:(c + 1) * kc].T.astype(BF16)
            ktaug[c, :, LANES:] = jnp.where(blk_lane == key_blk + c * bpc, 1.0, 0.0).astype(BF16)
            vt_c = vt_ref[0, :, c * kc:(c + 1) * kc]
            for hh in range(HEADS_PER_GROUP):
                own = jnp.logical_and(feat_row >= hh * HEAD_DIM, feat_row < (hh + 1) * HEAD_DIM)
                ones_row = ((hh + 1) % HEADS_PER_GROUP) * HEAD_DIM
                vtaug[hh, c] = jnp.where(own, vt_c, jnp.where(feat_row == ones_row, 1.0, 0.0)).astype(BF16)

    q2 = q_ref[...]
    lane = lax.broadcasted_iota(jnp.int32, q2.shape, 1)
    c_own = qi // bpc
    k_id = lax.broadcasted_iota(jnp.int32, (kc, blk_rows), 0)
    q_id = lax.broadcasted_iota(jnp.int32, (kc, blk_rows), 1)
    own_off = (qi - c_own * bpc) * blk_rows
    keep_own = jnp.logical_or(k_id - own_off <= q_id,
                              jnp.logical_or(k_id < own_off, k_id >= own_off + blk_rows))

    def group_max(s, m):
        sm = jnp.max(s.reshape(kc // SUBLANES, SUBLANES, blk_rows), axis=0)
        return sm if m is None else jnp.maximum(m, sm)

    q_augs, mruns = [], []
    for hh in range(HEADS_PER_GROUP):
        in_head = jnp.logical_and(lane >= hh * HEAD_DIM, lane < (hh + 1) * HEAD_DIM)
        qh = jnp.where(in_head, q2, 0.0)
        q_aug = jnp.concatenate([(qh * (scale * LOG2E)).astype(BF16), bias_ref[hh]], axis=1)
        s = jnp.where(keep_own, _dot_nt(ktaug[c_own], q_aug), NEG)
        sbuf[hh, c_own] = s
        q_augs.append(q_aug)
        mruns.append(group_max(s, None))

    def pairwise(n, body, carry):
        carry = lax.fori_loop(0, n // 2, lambda i, cr: body(2 * i + 1, body(2 * i, cr)), carry)
        return lax.fori_loop(0, n % 2, lambda _, cr: body(n - 1, cr), carry)

    def score_body(c, mruns):
        out = []
        for hh in range(HEADS_PER_GROUP):
            s = _dot_nt(ktaug[c], q_augs[hh])
            sbuf[hh, c] = s
            out.append(group_max(s, mruns[hh]))
        return tuple(out)

    mruns = pairwise(c_own, score_body, tuple(mruns))
    ms = [jnp.max(m, axis=0, keepdims=True) for m in mruns]

    def pv_body(c, accs):
        out = []
        for hh in range(HEADS_PER_GROUP):
            p = jnp.exp2(sbuf[hh, c] - ms[hh]).astype(BF16)
            out.append(accs[hh] + _dot(vtaug[hh, c], p))
        return tuple(out)

    zero = jnp.zeros((LANES, blk_rows), F32)
    accs = pairwise(c_own + 1, pv_body, (zero,) * HEADS_PER_GROUP)

    feat = lax.broadcasted_iota(jnp.int32, (LANES, blk_rows), 0)
    o_t = None
    for hh in range(HEADS_PER_GROUP):
        ones_row = ((hh + 1) % HEADS_PER_GROUP) * HEAD_DIM
        oh = accs[hh] / accs[hh][ones_row:ones_row + 1, :]
        o_t = oh if o_t is None else jnp.where(feat >= hh * HEAD_DIM, oh, o_t)
    o_ref[...] = o_t.T


def _moba_seq(q, bias, kt, vt, *, bpc=4):
    N, HD = q.shape
    B, _, T = kt.shape
    assert N == B * T and T % (bpc * MOBA_BLOCK) == 0 and HD % LANES == 0
    assert HEADS_PER_GROUP >= 2
    nb = T // MOBA_BLOCK
    nchunk = nb // bpc
    kc = bpc * MOBA_BLOCK
    return pl.pallas_call(
        functools.partial(_moba_seq_kernel, bpc=bpc, scale=HEAD_DIM ** -0.5),
        grid=(B, HD // LANES, nb),
        in_specs=[pl.BlockSpec((MOBA_BLOCK, LANES), lambda b, g, i: (b * nb + i, g)),
                  pl.BlockSpec((HEADS_PER_GROUP, MOBA_BLOCK, LANES), lambda b, g, i: (g, b * nb + i, 0)),
                  pl.BlockSpec((1, LANES, T), lambda b, g, i: (b, g, 0)),
                  pl.BlockSpec((1, LANES, T), lambda b, g, i: (b, g, 0))],
        out_specs=pl.BlockSpec((MOBA_BLOCK, LANES), lambda b, g, i: (b * nb + i, g)),
        out_shape=jax.ShapeDtypeStruct((N, HD), F32),
        scratch_shapes=[pltpu.VMEM((nchunk, kc, 2 * LANES), BF16),
                        pltpu.VMEM((HEADS_PER_GROUP, nchunk, LANES, kc), BF16),
                        pltpu.VMEM((HEADS_PER_GROUP, nchunk, kc, MOBA_BLOCK), F32)],
        compiler_params=pltpu.CompilerParams(
            dimension_semantics=("arbitrary", "arbitrary", "arbitrary"),
            vmem_limit_bytes=VMEM_LIMIT),
        name="moba_seq",
    )(q, bias, kt, vt)


def _moba_step_kernel(pt_ref, qt_ref, knt_ref, vnt_ref, ck_hbm, cv_hbm, ot_ref,
                      kbuf, vbuf, s_buf, ksem, vsem, *, n_pages, page, nb, scale):
    b = pl.program_id(0)
    nseq = pl.num_programs(0)
    slot = b % 2
    HD = qt_ref.shape[0]

    def page_copy(src_hbm, dst, sem, seq, sl, j):
        return pltpu.make_async_copy(src_hbm.at[pt_ref[seq, j]], dst.at[sl, j], sem.at[sl])

    def start_fetch(seq, sl):
        for j in range(n_pages):
            page_copy(ck_hbm, kbuf, ksem, seq, sl, j).start()
        for j in range(n_pages):
            page_copy(cv_hbm, vbuf, vsem, seq, sl, j).start()

    @pl.when(b == 0)
    def _():
        start_fetch(b, slot)
        ot_ref[...] = jnp.zeros(ot_ref.shape, F32)

    @pl.when(b + 1 < nseq)
    def _():
        start_fetch(b + 1, 1 - slot)

    seq_lane = lax.broadcasted_iota(jnp.int32, qt_ref.shape, 1) == b
    column = lambda ref: jnp.sum(jnp.where(seq_lane, ref[...], 0.0), axis=1, keepdims=True)
    q_col = column(qt_ref) * scale
    kn_col = column(knt_ref)
    vn_col = column(vnt_ref)
    q_b = jnp.broadcast_to(q_col, (HD, page))

    def head_sums(x):
        return jnp.sum(x.reshape(N_HEADS, HEAD_DIM, x.shape[1]), axis=1)

    def head_bcast(x):
        return jnp.broadcast_to(x[:, None, :], (N_HEADS, HEAD_DIM, x.shape[1])).reshape(HD, x.shape[1])

    for j in range(n_pages):
        page_copy(ck_hbm, kbuf, ksem, b, slot, j).wait()
    for j in range(n_pages):
        s_buf[:, j * page:(j + 1) * page] = head_sums(kbuf[slot, j] * q_b)

    gate = _block_sums(s_buf[...], nb) * (1.0 / MOBA_BLOCK)
    sel = _topk_select(gate, jnp.ones(gate.shape, jnp.bool_))

    s_own = head_sums(q_col * kn_col)
    m = s_own
    for j in range(nb):
        sj = jnp.where(sel[:, j:j + 1], s_buf[:, j * MOBA_BLOCK:(j + 1) * MOBA_BLOCK], NEG)
        s_buf[:, j * MOBA_BLOCK:(j + 1) * MOBA_BLOCK] = sj
        m = jnp.maximum(m, jnp.max(sj, axis=1, keepdims=True))
    p_own = jnp.exp(s_own - m)
    p = jnp.exp(s_buf[...] - m)
    l = p_own + jnp.sum(p, axis=1, keepdims=True)

    for j in range(n_pages):
        page_copy(cv_hbm, vbuf, vsem, b, slot, j).wait()
    acc = jnp.zeros((HD, page), F32)
    for j in range(n_pages):
        acc = acc + vbuf[slot, j] * head_bcast(p[:, j * page:(j + 1) * page])
    o_col = jnp.sum(acc, axis=1, keepdims=True) + head_bcast(p_own) * vn_col
    o_col = o_col / head_bcast(l)
    ot_ref[...] = jnp.where(seq_lane, o_col, ot_ref[...])


def _moba_step(qt, knt, vnt, ck, cv, page_table):
    HD, N = qt.shape
    page = ck.shape[2]
    n_pages = page_table.shape[1]
    past_len = n_pages * page
    assert past_len % MOBA_BLOCK == 0 and MOBA_BLOCK % page == 0 and page % LANES == 0
    nb = past_len // MOBA_BLOCK
    assert nb >= 1
    full = pl.BlockSpec((HD, N), lambda b, pt: (0, 0))
    return pl.pallas_call(
        functools.partial(_moba_step_kernel, n_pages=n_pages, page=page, nb=nb,
                          scale=HEAD_DIM ** -0.5),
        grid_spec=pltpu.PrefetchScalarGridSpec(
            num_scalar_prefetch=1,
            grid=(N,),
            in_specs=[full, full, full,
                      pl.BlockSpec(memory_space=pl.ANY),
                      pl.BlockSpec(memory_space=pl.ANY)],
            out_specs=full,
            scratch_shapes=[pltpu.VMEM((2, n_pages, HD, page), F32),
                            pltpu.VMEM((2, n_pages, HD, page), F32),
                            pltpu.VMEM((N_HEADS, past_len), F32),
                            pltpu.SemaphoreType.DMA((2,)),
                            pltpu.SemaphoreType.DMA((2,))]),
        out_shape=jax.ShapeDtypeStruct((HD, N), F32),
        compiler_params=pltpu.CompilerParams(
            dimension_semantics=("arbitrary",), vmem_limit_bytes=VMEM_LIMIT),
        name="moba_step",
    )(page_table, qt, knt, vnt, ck, cv)


def kernel(x_prompt, x_sample, state_conv, state_h, cache_k, cache_v, page_table, norm_mix, norm_mlp, w_ff1, w_ff2, w_rg_in, b_rg_in, conv_w, conv_b, w_gate_a, b_gate_a, w_gate_i, b_gate_i, lru_lambda, w_rg_out, b_rg_out, norm_kv, w_kv, w_q, w_o, norm_out):
    B, T, D = x_prompt.shape
    NS, TS, _ = x_sample.shape
    depth = norm_mix.shape[0]
    assert depth == 2 and w_rg_in.shape[0] == 1 and w_q.shape[0] == 1 and TS == 1
    HD = N_HEADS * HEAD_DIM
    d_rnn = w_rg_in.shape[2] // 2
    hist = CONV_W - 1
    n_phys, page = cache_k.shape[0], cache_k.shape[1]
    bf = lambda w: w.astype(BF16)

    rg = (norm_mix[0], bf(w_rg_in[0]), b_rg_in[0], conv_w[0], conv_b[0],
          bf(w_gate_a[0]), b_gate_a[0].reshape(-1), bf(w_gate_i[0]), b_gate_i[0].reshape(-1),
          lru_lambda[0], bf(w_rg_out[0]), b_rg_out[0])
    mlp0 = (norm_mlp[0], bf(w_ff1[0]), bf(w_ff2[0]), norm_kv, bf(w_kv.T), norm_mix[1], bf(w_q[0]))
    mlp1 = (bf(w_o[0]), norm_mlp[1], bf(w_ff1[1]), bf(w_ff2[1]), norm_out)

    def heads_last(xt):
        lead = xt.shape[:-2]
        xt = xt.reshape(lead + (N_HEADS, HEAD_DIM, xt.shape[-1]))
        return jnp.moveaxis(xt, -1, -3)

    x1, conv_p, h_p = _rglru_seq(x_prompt, jnp.zeros((B, hist, d_rnn), F32),
                                 jnp.zeros((B, d_rnn), F32), *rg, tm=256)
    x2, kt_p, vt_p, q_p = _mlp_kv(x1, B, *mlp0, tm=256)
    attn_p = _moba_seq(q_p, _moba_gate(q_p, kt_p, tq=min(T, 1024)), kt_p, vt_p)
    y_p = _attn_out_mlp(x2, attn_p, *mlp1, tm=256)

    xs = x_sample.reshape(NS, D)
    x1s, conv_s, h_s = _rglru_step(xs, jnp.swapaxes(state_conv[0], 0, 1), state_h[0], *rg)
    x2s, kt_s, vt_s, q_s = _mlp_kv(x1s, 1, *mlp0, tm=NS)
    ck = jnp.transpose(cache_k, (0, 2, 3, 1)).reshape(n_phys, HD, page)
    cv = jnp.transpose(cache_v, (0, 2, 3, 1)).reshape(n_phys, HD, page)
    attn_s = _moba_step(q_s.T, kt_s[0], vt_s[0], ck, cv, page_table).T
    y_s = _attn_out_mlp(x2s, attn_s, *mlp1, tm=NS)

    return (y_p.reshape(B, T, D), y_s.reshape(NS, 1, D),
            conv_p[None], h_p[None], heads_last(kt_p), heads_last(vt_p),
            jnp.swapaxes(conv_s, 0, 1)[None], h_s[None],
            heads_last(kt_s[0])[:, None], heads_last(vt_s[0])[:, None])
```

```python
import functools
import math

import jax
import jax.numpy as jnp
from jax import lax
from jax.experimental import pallas as pl
from jax.experimental.pallas import tpu as pltpu

N_HEADS = 16
HEAD_DIM = 64
N_LRU_BLOCKS = 4
CONV_W = 4
LRU_C = 8.0
MOBA_BLOCK = 256
MOBA_TOPK = 3
EPS = 1e-6
NEG = -1e30
LOG2E = math.log2(math.e)

LANES = 128
SUBLANES = 8
HEADS_PER_GROUP = LANES // HEAD_DIM
VMEM_LIMIT = 56 * 1024 * 1024

BF16 = jnp.bfloat16
F32 = jnp.float32


def _rmsnorm(x, g):
    return x * lax.rsqrt(jnp.mean(x * x, axis=-1, keepdims=True) + EPS) * g


def _rmsnorm_mxu(x, g):
    d = x.shape[1]
    ss = jnp.dot((x * x).astype(BF16), jnp.ones((d, LANES), BF16), preferred_element_type=F32)
    inv = lax.rsqrt(ss * (1.0 / d) + EPS)
    return x * jnp.concatenate([inv] * (d // LANES), axis=1) * g


def _dot(a, b, precision=None):
    return jnp.dot(a, b, preferred_element_type=F32, precision=precision)


def _dot_nt(a, b):
    return lax.dot_general(a, b, (((1,), (1,)), ((), ())), preferred_element_type=F32)


def _gelu_tanh(x):
    c = math.sqrt(2.0 / math.pi)
    return x * (0.5 * (1.0 + jnp.tanh(c * (x + 0.044715 * (x * x * x)))))


def _softplus(y):
    return jnp.maximum(y, 0.0) + jnp.log1p(jnp.exp(-jnp.abs(y)))


def _neg_expm1(x):
    return jnp.tanh(-0.5 * x) * (jnp.exp(x) + 1.0)


def _lru_coeffs(c, wga_ref, bga, wgi_ref, bgi, lam):
    blk = c.shape[1] // N_LRU_BLOCKS
    cb = c.astype(BF16)
    ra, ri = [], []
    for n in range(N_LRU_BLOCKS):
        cn = cb[:, n * blk:(n + 1) * blk]
        ra.append(_dot(cn, wga_ref[n]))
        ri.append(_dot(cn, wgi_ref[n]))
    r = jax.nn.sigmoid(jnp.concatenate(ra, axis=1) + bga)
    i = jax.nn.sigmoid(jnp.concatenate(ri, axis=1) + bgi)
    log_a = -LRU_C * r * _softplus(-lam)
    a = jnp.exp(log_a)
    mult = jnp.sqrt(_neg_expm1(2.0 * log_a))
    return a, mult * (i * c)


def _rglru_seq_kernel(x_ref, conv0_ref, h0_ref, g_ref, win_ref, bin_ref, cw_ref, cb_ref,
                      wga_ref, bga_ref, wgi_ref, bgi_ref, lam_ref, wout_ref, bout_ref,
                      x1_ref, convn_ref, hn_ref,
                      ubuf, aloc, hloc, hcar, *, tm, d_rnn):
    t = pl.program_id(1)
    nt = pl.num_programs(1)
    hist = CONV_W - 1

    @pl.when(t == 0)
    def _():
        ubuf[SUBLANES - hist:SUBLANES, :] = conv0_ref[0]
        hcar[...] = h0_ref[0]

    x = x_ref[...]
    xn = _rmsnorm_mxu(x, g_ref[...]).astype(BF16)
    proj = _dot(xn, win_ref[...]) + bin_ref[...]
    gate = _gelu_tanh(proj[:, :d_rnn])
    u = proj[:, d_rnn:]
    ubuf[SUBLANES:SUBLANES + tm, :] = u

    c = cb_ref[...] + ubuf[SUBLANES - hist:SUBLANES - hist + tm, :] * cw_ref[0:1, :]
    for k in range(1, CONV_W):
        off = SUBLANES - hist + k
        c = c + ubuf[off:off + tm, :] * cw_ref[k:k + 1, :]

    a, b = _lru_coeffs(c, wga_ref, bga_ref[...], wgi_ref, bgi_ref[...], lam_ref[...])

    a = a.reshape(tm // SUBLANES, SUBLANES, d_rnn)
    b = b.reshape(tm // SUBLANES, SUBLANES, d_rnn)
    row = lax.broadcasted_iota(jnp.int32, a.shape, 1)
    for s in (1, 2, 4):
        ok = row >= s
        a_sh = pltpu.roll(a, s, 1)
        b_sh = pltpu.roll(b, s, 1)
        b = jnp.where(ok, a * b_sh + b, b)
        a = jnp.where(ok, a * a_sh, a)
    aloc[...] = a.reshape(tm, d_rnn)
    hloc[...] = b.reshape(tm, d_rnn)

    def body(gi, carry):
        r0 = pl.multiple_of(gi * SUBLANES, SUBLANES)
        h = hloc[pl.ds(r0, SUBLANES), :] + aloc[pl.ds(r0, SUBLANES), :] * carry
        hloc[pl.ds(r0, SUBLANES), :] = h
        return h[SUBLANES - 1:SUBLANES, :]

    hcar[...] = lax.fori_loop(0, tm // SUBLANES, body, hcar[...])

    hg = (hloc[...] * gate).astype(BF16)
    x1_ref[...] = x + _dot(hg, wout_ref[...]) + bout_ref[...]

    ubuf[SUBLANES - hist:SUBLANES, :] = ubuf[SUBLANES + tm - hist:SUBLANES + tm, :]

    @pl.when(t == nt - 1)
    def _():
        convn_ref[0] = ubuf[SUBLANES - hist:SUBLANES, :]
        hn_ref[0] = hcar[...]


def _rglru_seq(x, conv0, h0, g, w_in, b_in, cw, cb, wga, bga, wgi, bgi, lam, w_out, b_out, *, tm):
    B, T, D = x.shape
    d_rnn = w_in.shape[1] // 2
    hist = CONV_W - 1
    assert T % tm == 0 and tm % SUBLANES == 0 and tm >= hist
    nt = T // tm
    x2 = x.reshape(B * T, D)
    full = lambda a: pl.BlockSpec(a.shape, lambda b, t: (0,) * a.ndim)
    row = lambda v: v.reshape(1, -1)
    args = (x2, conv0, h0.reshape(B, 1, d_rnn), row(g), w_in, row(b_in), cw, row(cb),
            wga, row(bga), wgi, row(bgi), row(lam), w_out, row(b_out))
    in_specs = [pl.BlockSpec((tm, D), lambda b, t: (b * nt + t, 0)),
                pl.BlockSpec((1, hist, d_rnn), lambda b, t: (b, 0, 0)),
                pl.BlockSpec((1, 1, d_rnn), lambda b, t: (b, 0, 0))]
    in_specs += [full(a) for a in args[3:]]
    x1, convn, hn = pl.pallas_call(
        functools.partial(_rglru_seq_kernel, tm=tm, d_rnn=d_rnn),
        grid=(B, nt),
        in_specs=in_specs,
        out_specs=[pl.BlockSpec((tm, D), lambda b, t: (b * nt + t, 0)),
                   pl.BlockSpec((1, hist, d_rnn), lambda b, t: (b, 0, 0)),
                   pl.BlockSpec((1, 1, d_rnn), lambda b, t: (b, 0, 0))],
        out_shape=[jax.ShapeDtypeStruct((B * T, D), F32),
                   jax.ShapeDtypeStruct((B, hist, d_rnn), F32),
                   jax.ShapeDtypeStruct((B, 1, d_rnn), F32)],
        scratch_shapes=[pltpu.VMEM((tm + SUBLANES, d_rnn), F32),
                        pltpu.VMEM((tm, d_rnn), F32),
                        pltpu.VMEM((tm, d_rnn), F32),
                        pltpu.VMEM((1, d_rnn), F32)],
        compiler_params=pltpu.CompilerParams(
            dimension_semantics=("arbitrary", "arbitrary"), vmem_limit_bytes=VMEM_LIMIT),
        name="rglru_seq",
    )(*args)
    return x1, convn, hn.reshape(B, d_rnn)


def _rglru_step_kernel(x_ref, conv_ref, h0_ref, g_ref, win_ref, bin_ref, cw_ref, cb_ref,
                       wga_ref, bga_ref, wgi_ref, bgi_ref, lam_ref, wout_ref, bout_ref,
                       x1_ref, convn_ref, hn_ref, *, d_rnn):
    hist = CONV_W - 1
    x = x_ref[...]
    xn = _rmsnorm(x, g_ref[...]).astype(BF16)
    proj = _dot(xn, win_ref[...]) + bin_ref[...]
    gate = _gelu_tanh(proj[:, :d_rnn])
    u = proj[:, d_rnn:]
    c = cb_ref[...] + conv_ref[0] * cw_ref[0:1, :]
    for k in range(1, hist):
        c = c + conv_ref[k] * cw_ref[k:k + 1, :]
    c = c + u * cw_ref[hist:hist + 1, :]
    a, b = _lru_coeffs(c, wga_ref, bga_ref[...], wgi_ref, bgi_ref[...], lam_ref[...])
    h = a * h0_ref[...] + b
    hn_ref[...] = h
    for k in range(hist - 1):
        convn_ref[k] = conv_ref[k + 1]
    convn_ref[hist - 1] = u
    x1_ref[...] = x + _dot((h * gate).astype(BF16), wout_ref[...]) + bout_ref[...]


def _rglru_step(x, conv0, h0, g, w_in, b_in, cw, cb, wga, bga, wgi, bgi, lam, w_out, b_out):
    N, D = x.shape
    d_rnn = w_in.shape[1] // 2
    hist = CONV_W - 1
    row = lambda v: v.reshape(1, -1)
    args = (x, conv0, h0, row(g), w_in, row(b_in), cw, row(cb),
            wga, row(bga), wgi, row(bgi), row(lam), w_out, row(b_out))
    return pl.pallas_call(
        functools.partial(_rglru_step_kernel, d_rnn=d_rnn),
        out_shape=[jax.ShapeDtypeStruct((N, D), F32),
                   jax.ShapeDtypeStruct((hist, N, d_rnn), F32),
                   jax.ShapeDtypeStruct((N, d_rnn), F32)],
        compiler_params=pltpu.CompilerParams(vmem_limit_bytes=VMEM_LIMIT),
        name="rglru_step",
    )(*args)


def _sqrelu_mlp(xn_bf, w1_ref, w2_ref, fc):
    d_ff = w1_ref.shape[1]
    acc = None
    for c0 in range(0, d_ff, fc):
        h = jnp.maximum(_dot(xn_bf, w1_ref[:, c0:c0 + fc]), 0.0)
        part = _dot((h * h).astype(BF16), w2_ref[c0:c0 + fc, :])
        acc = part if acc is None else acc + part
    return acc


def _mlp_kv_kernel(x_ref, gm_ref, w1_ref, w2_ref, gkv_ref, wkvt_ref, gq_ref, wq_ref,
                   x2_ref, kt_ref, vt_ref, q_ref, *, fc):
    x = x_ref[...]
    x2 = x + _sqrelu_mlp(_rmsnorm(x, gm_ref[...]).astype(BF16), w1_ref, w2_ref, fc)
    x2_ref[...] = x2
    hd = kt_ref.shape[1]
    kvt = _dot_nt(wkvt_ref[...], _rmsnorm(x2, gkv_ref[...]).astype(BF16))
    kt_ref[0] = kvt[:hd, :]
    vt_ref[0] = kvt[hd:, :]
    q_ref[...] = _dot(_rmsnorm(x2, gq_ref[...]).astype(BF16), wq_ref[...])


def _mlp_kv(x, B, gm, w1, w2, gkv, wkvt, gq, wq, *, tm, fc=512):
    N, D = x.shape
    hd = wq.shape[1]
    T = N // B
    assert N == B * T and T % tm == 0
    nt = T // tm
    row = lambda v: v.reshape(1, -1)
    args = (x, row(gm), w1, w2, row(gkv), wkvt, row(gq), wq)
    full = lambda a: pl.BlockSpec(a.shape, lambda i: (0,) * a.ndim)
    tile = lambda w: pl.BlockSpec((tm, w), lambda i: (i, 0))
    tile_t = pl.BlockSpec((1, hd, tm), lambda i: (i // nt, 0, i % nt))
    return pl.pallas_call(
        functools.partial(_mlp_kv_kernel, fc=fc),
        grid=(N // tm,),
        in_specs=[tile(D)] + [full(a) for a in args[1:]],
        out_specs=[tile(D), tile_t, tile_t, tile(hd)],
        out_shape=[jax.ShapeDtypeStruct((N, D), F32),
                   jax.ShapeDtypeStruct((B, hd, T), F32),
                   jax.ShapeDtypeStruct((B, hd, T), F32),
                   jax.ShapeDtypeStruct((N, hd), F32)],
        compiler_params=pltpu.CompilerParams(
            dimension_semantics=("arbitrary",), vmem_limit_bytes=VMEM_LIMIT),
        name="mlp_kv",
    )(*args)


def _attn_out_mlp_kernel(x_ref, a_ref, wo_ref, gm_ref, w1_ref, w2_ref, go_ref, y_ref, *, fc):
    x3 = x_ref[...] + _dot(a_ref[...].astype(BF16), wo_ref[...])
    x4 = x3 + _sqrelu_mlp(_rmsnorm(x3, gm_ref[...]).astype(BF16), w1_ref, w2_ref, fc)
    y_ref[...] = _rmsnorm(x4, go_ref[...])


def _attn_out_mlp(x, attn, wo, gm, w1, w2, go, *, tm, fc=512):
    N, D = x.shape
    assert N % tm == 0
    row = lambda v: v.reshape(1, -1)
    args = (x, attn, wo, row(gm), w1, w2, row(go))
    full = lambda a: pl.BlockSpec(a.shape, lambda i: (0,) * a.ndim)
    tile = lambda w: pl.BlockSpec((tm, w), lambda i: (i, 0))
    return pl.pallas_call(
        functools.partial(_attn_out_mlp_kernel, fc=fc),
        grid=(N // tm,),
        in_specs=[tile(D), tile(attn.shape[1])] + [full(a) for a in args[2:]],
        out_specs=tile(D),
        out_shape=jax.ShapeDtypeStruct((N, D), F32),
        compiler_params=pltpu.CompilerParams(
            dimension_semantics=("arbitrary",), vmem_limit_bytes=VMEM_LIMIT),
        name="attn_out_mlp",
    )(*args)


def _topk_select(gate, allowed, axis=1):
    nb = gate.shape[axis]
    blk = lax.broadcasted_iota(jnp.int32, gate.shape, axis)
    g = jnp.where(allowed, gate, NEG)
    sel = jnp.zeros(gate.shape, jnp.bool_)
    for _ in range(MOBA_TOPK):
        m = jnp.max(g, axis=axis, keepdims=True)
        first = jnp.min(jnp.where(g == m, blk, nb), axis=axis, keepdims=True)
        pick = blk == first
        sel = jnp.logical_or(sel, pick)
        g = jnp.where(pick, -jnp.inf, g)
    return jnp.logical_and(sel, allowed)


def _block_sums(x, nb):
    blk = lax.broadcasted_iota(jnp.int32, (x.shape[0], nb), 1)
    out = jnp.zeros((x.shape[0], nb), F32)
    for j in range(nb):
        sj = jnp.sum(x[:, j * MOBA_BLOCK:(j + 1) * MOBA_BLOCK], axis=1, keepdims=True)
        out = jnp.where(blk == j, sj, out)
    return out


def _moba_gate_kernel(q_ref, kt_ref, bias_ref, kw, *, nb, nbp, tq):
    t = pl.program_id(2)
    blk_rows = MOBA_BLOCK

    @pl.when(t == 0)
    def _():
        km_t = _block_sums(kt_ref[0], nb) * (1.0 / blk_rows)
        km_t = jnp.concatenate([km_t, jnp.zeros((LANES, LANES - nb), F32)], axis=1)
        km = km_t.T[0:nbp, :]
        feat = lax.broadcasted_iota(jnp.int32, km.shape, 1)
        for hh in range(HEADS_PER_GROUP):
            own = jnp.logical_and(feat >= hh * HEAD_DIM, feat < (hh + 1) * HEAD_DIM)
            k_h = jnp.where(own, km, 0.0)
            k_hi = k_h.astype(BF16)
            k_lo = (k_h - k_hi.astype(F32)).astype(BF16)
            kw[hh, 0:nbp, :] = jnp.concatenate([k_hi, k_hi], axis=1)
            kw[hh, nbp:, :] = jnp.concatenate([k_lo, jnp.zeros_like(k_lo)], axis=1)

    q2 = q_ref[...]
    q_hi = q2.astype(BF16)
    q_lo = (q2 - q_hi.astype(F32)).astype(BF16)
    q_cat = jnp.concatenate([q_hi, q_lo], axis=1)
    blk_t = lax.broadcasted_iota(jnp.int32, (nbp, tq), 0)
    q_blk = (t * tq + lax.broadcasted_iota(jnp.int32, (nbp, tq), 1)) // blk_rows
    for hh in range(HEADS_PER_GROUP):
        parts = _dot_nt(kw[hh], q_cat)
        gate_t = parts[0:nbp, :] + parts[nbp:, :]
        sel_t = _topk_select(gate_t, blk_t < q_blk, axis=0)
        bias_t = jnp.where(jnp.logical_or(sel_t, blk_t == q_blk), 0.0, NEG)
        bias_t = jnp.concatenate([bias_t, jnp.zeros((LANES - nbp, tq), F32)], axis=0)
        bias_ref[hh] = bias_t.T.astype(BF16)


def _moba_gate(q, kt, *, tq=1024):
    N, HD = q.shape
    B, _, T = kt.shape
    assert N == B * T and T % tq == 0 and tq % MOBA_BLOCK == 0 and HD % LANES == 0
    nb = T // MOBA_BLOCK
    nbp = -(-nb // (2 * SUBLANES)) * (2 * SUBLANES)
    assert nbp <= LANES
    nt = T // tq
    return pl.pallas_call(
        functools.partial(_moba_gate_kernel, nb=nb, nbp=nbp, tq=tq),
        grid=(B, HD // LANES, nt),
        in_specs=[pl.BlockSpec((tq, LANES), lambda b, g, t: (b * nt + t, g)),
                  pl.BlockSpec((1, LANES, T), lambda b, g, t: (b, g, 0))],
        out_specs=pl.BlockSpec((HEADS_PER_GROUP, tq, LANES), lambda b, g, t: (g, b * nt + t, 0)),
        out_shape=jax.ShapeDtypeStruct((HD // HEAD_DIM, N, LANES), BF16),
        scratch_shapes=[pltpu.VMEM((HEADS_PER_GROUP, 2 * nbp, 2 * LANES), BF16)],
        compiler_params=pltpu.CompilerParams(
            dimension_semantics=("arbitrary", "arbitrary", "arbitrary"),
            vmem_limit_bytes=VMEM_LIMIT),
        name="moba_gate",
    )(q, kt)


def _moba_seq_kernel(q_ref, bias_ref, kt_ref, vt_ref, o_ref, ktaug, vtaug, sbuf_even, sbuf_odd, msbuf, *,
                     nb, bpc, scale):
    i = pl.program_id(2)
    blk_rows = MOBA_BLOCK
    kc = bpc * blk_rows
    nchunk = ktaug.shape[0]
    c_own = i // bpc
    heads = range(HEADS_PER_GROUP)

    @pl.when(i == 0)
    def _():
        key_blk = lax.broadcasted_iota(jnp.int32, (kc, LANES), 0) // blk_rows
        blk_lane = lax.broadcasted_iota(jnp.int32, (kc, LANES), 1)
        feat_row = lax.broadcasted_iota(jnp.int32, (LANES, kc), 0)
        for c in range(nchunk):
            ktaug[c, :, 0:LANES] = kt_ref[0, :, c * kc:(c + 1) * kc].T.astype(BF16)
            ktaug[c, :, LANES:] = jnp.where(blk_lane == key_blk + c * bpc, 1.0, 0.0).astype(BF16)
            vt_c = vt_ref[0, :, c * kc:(c + 1) * kc]
            for hh in heads:
                own = jnp.logical_and(feat_row >= hh * HEAD_DIM, feat_row < (hh + 1) * HEAD_DIM)
                ones_row = ((hh + 1) % HEADS_PER_GROUP) * HEAD_DIM
                vtaug[hh, c] = jnp.where(own, vt_c, jnp.where(feat_row == ones_row, 1.0, 0.0)).astype(BF16)

    def group_max(s, m):
        sm = jnp.max(s.reshape(kc // SUBLANES, SUBLANES, blk_rows), axis=0)
        return sm if m is None else jnp.maximum(m, sm)

    def pairwise(n, body, carry):
        carry = lax.fori_loop(0, n // 2, lambda t, cr: body(2 * t + 1, body(2 * t, cr)), carry)
        return lax.fori_loop(0, n % 2, lambda _, cr: body(n - 1, cr), carry)

    def make_q_augs():
        q2 = q_ref[...]
        lane = lax.broadcasted_iota(jnp.int32, q2.shape, 1)
        out = []
        for hh in heads:
            in_head = jnp.logical_and(lane >= hh * HEAD_DIM, lane < (hh + 1) * HEAD_DIM)
            qs = (jnp.where(in_head, q2, 0.0) * (scale * LOG2E)).astype(BF16)
            out.append(jnp.concatenate([qs, bias_ref[hh]], axis=1))
        return out

    def own_chunk_scores(s_cur, q_augs):
        k_id = lax.broadcasted_iota(jnp.int32, (kc, blk_rows), 0)
        q_id = lax.broadcasted_iota(jnp.int32, (kc, blk_rows), 1)
        own_off = (i - c_own * bpc) * blk_rows
        keep = jnp.logical_or(k_id - own_off <= q_id,
                              jnp.logical_or(k_id < own_off, k_id >= own_off + blk_rows))
        mruns = []
        for hh in heads:
            s = jnp.where(keep, _dot_nt(ktaug[c_own], q_augs[hh]), NEG)
            s_cur[hh, c_own] = s
            mruns.append(group_max(s, None))
        return tuple(mruns)

    def score_chunk(s_cur, c, q_augs, mruns):
        out = []
        for hh in heads:
            s = _dot_nt(ktaug[c], q_augs[hh])
            s_cur[hh, c] = s
            out.append(group_max(s, mruns[hh]))
        return tuple(out)

    def pv_chunk(s_prv, c, ms, accs):
        return tuple(accs[hh] + _dot(vtaug[hh, c], jnp.exp2(s_prv[hh, c] - ms[hh]).astype(BF16))
                     for hh in heads)

    def store_max(cur, mruns):
        for hh in heads:
            msbuf[cur, hh] = jnp.max(mruns[hh], axis=0, keepdims=True)

    def store_out(accs):
        feat = lax.broadcasted_iota(jnp.int32, (LANES, blk_rows), 0)
        o_t = None
        for hh in heads:
            ones_row = ((hh + 1) % HEADS_PER_GROUP) * HEAD_DIM
            oh = accs[hh] / accs[hh][ones_row:ones_row + 1, :]
            o_t = oh if o_t is None else jnp.where(feat >= hh * HEAD_DIM, oh, o_t)
        o_ref[...] = o_t.T

    zero_acc = (jnp.zeros((LANES, blk_rows), F32),) * HEADS_PER_GROUP

    def step(cur, s_cur, s_prv):
        prv = 1 - cur
        parity = i % 2 == cur

        @pl.when(jnp.logical_and(parity, i == 0))
        def _():
            store_max(cur, own_chunk_scores(s_cur, make_q_augs()))
            o_ref[...] = jnp.zeros(o_ref.shape, F32)

        @pl.when(jnp.logical_and(parity, jnp.logical_and(i > 0, i < nb)))
        def _():
            q_augs = make_q_augs()
            ms = [msbuf[prv, hh] for hh in heads]
            c_last = (i - 1) // bpc
            mruns = own_chunk_scores(s_cur, q_augs)
            accs = pv_chunk(s_prv, c_last, ms, zero_acc)

            def both(c, carry):
                mruns, accs = carry
                return score_chunk(s_cur, c, q_augs, mruns), pv_chunk(s_prv, c, ms, accs)

            mruns, accs = pairwise(c_last, both, (mruns, accs))
            n_extra = jnp.where(i % bpc == 0, 1, 0)
            mruns = lax.fori_loop(0, n_extra, lambda _, m: score_chunk(s_cur, c_last, q_augs, m), mruns)
            store_max(cur, mruns)
            store_out(accs)

        @pl.when(jnp.logical_and(parity, i == nb))
        def _():
            ms = [msbuf[prv, hh] for hh in heads]
            store_out(pairwise(nchunk, lambda c, a: pv_chunk(s_prv, c, ms, a), zero_acc))

    step(0, sbuf_even, sbuf_odd)
    step(1, sbuf_odd, sbuf_even)


def _moba_seq(q, bias, kt, vt, *, bpc=4):
    N, HD = q.shape
    B, _, T = kt.shape
    assert N == B * T and T % (bpc * MOBA_BLOCK) == 0 and HD % LANES == 0
    assert HEADS_PER_GROUP >= 2
    nb = T // MOBA_BLOCK
    nchunk = nb // bpc
    kc = bpc * MOBA_BLOCK
    q_blk = lambda b, g, i: (b * nb + jnp.minimum(i, nb - 1), g)
    o_blk = lambda b, g, i: (b * nb + jnp.maximum(i - 1, 0), g)
    resident = dict(pipeline_mode=pl.Buffered(1))
    return pl.pallas_call(
        functools.partial(_moba_seq_kernel, nb=nb, bpc=bpc, scale=HEAD_DIM ** -0.5),
        grid=(B, HD // LANES, nb + 1),
        in_specs=[pl.BlockSpec((MOBA_BLOCK, LANES), q_blk),
                  pl.BlockSpec((HEADS_PER_GROUP, MOBA_BLOCK, LANES),
                               lambda b, g, i: (g, b * nb + jnp.minimum(i, nb - 1), 0)),
                  pl.BlockSpec((1, LANES, T), lambda b, g, i: (b, g, 0), **resident),
                  pl.BlockSpec((1, LANES, T), lambda b, g, i: (b, g, 0), **resident)],
        out_specs=pl.BlockSpec((MOBA_BLOCK, LANES), o_blk),
        out_shape=jax.ShapeDtypeStruct((N, HD), F32),
        scratch_shapes=[pltpu.VMEM((nchunk, kc, 2 * LANES), BF16),
                        pltpu.VMEM((HEADS_PER_GROUP, nchunk, LANES, kc), BF16),
                        pltpu.VMEM((HEADS_PER_GROUP, nchunk, kc, MOBA_BLOCK), F32),
                        pltpu.VMEM((HEADS_PER_GROUP, nchunk, kc, MOBA_BLOCK), F32),
                        pltpu.VMEM((2, HEADS_PER_GROUP, 1, MOBA_BLOCK), F32)],
        compiler_params=pltpu.CompilerParams(
            dimension_semantics=("arbitrary", "arbitrary", "arbitrary"),
            vmem_limit_bytes=VMEM_LIMIT),
        name="moba_seq",
    )(q, bias, kt, vt)


def _moba_step_kernel(pt_ref, qt_ref, knt_ref, vnt_ref, ck_hbm, cv_hbm, ot_ref,
                      kbuf, vbuf, s_buf, ksem, vsem, *, n_pages, page, nb, scale):
    b = pl.program_id(0)
    nseq = pl.num_programs(0)
    slot = b % 2
    HD = qt_ref.shape[0]

    def page_copy(src_hbm, dst, sem, seq, sl, j):
        return pltpu.make_async_copy(src_hbm.at[pt_ref[seq, j]], dst.at[sl, j], sem.at[sl])

    def start_fetch(seq, sl):
        for j in range(n_pages):
            page_copy(ck_hbm, kbuf, ksem, seq, sl, j).start()
        for j in range(n_pages):
            page_copy(cv_hbm, vbuf, vsem, seq, sl, j).start()

    @pl.when(b == 0)
    def _():
        start_fetch(b, slot)
        ot_ref[...] = jnp.zeros(ot_ref.shape, F32)

    @pl.when(b + 1 < nseq)
    def _():
        start_fetch(b + 1, 1 - slot)

    seq_lane = lax.broadcasted_iota(jnp.int32, qt_ref.shape, 1) == b
    column = lambda ref: jnp.sum(jnp.where(seq_lane, ref[...], 0.0), axis=1, keepdims=True)
    q_col = column(qt_ref) * scale
    kn_col = column(knt_ref)
    vn_col = column(vnt_ref)
    q_b = jnp.broadcast_to(q_col, (HD, page))

    def head_sums(x):
        return jnp.sum(x.reshape(N_HEADS, HEAD_DIM, x.shape[1]), axis=1)

    def head_bcast(x):
        return jnp.broadcast_to(x[:, None, :], (N_HEADS, HEAD_DIM, x.shape[1])).reshape(HD, x.shape[1])

    for j in range(n_pages):
        page_copy(ck_hbm, kbuf, ksem, b, slot, j).wait()
    for j in range(n_pages):
        s_buf[:, j * page:(j + 1) * page] = head_sums(kbuf[slot, j] * q_b)

    gate = _block_sums(s_buf[...], nb) * (1.0 / MOBA_BLOCK)
    sel = _topk_select(gate, jnp.ones(gate.shape, jnp.bool_))

    s_own = head_sums(q_col * kn_col)
    m = s_own
    for j in range(nb):
        sj = jnp.where(sel[:, j:j + 1], s_buf[:, j * MOBA_BLOCK:(j + 1) * MOBA_BLOCK], NEG)
        s_buf[:, j * MOBA_BLOCK:(j + 1) * MOBA_BLOCK] = sj
        m = jnp.maximum(m, jnp.max(sj, axis=1, keepdims=True))
    p_own = jnp.exp(s_own - m)
    p = jnp.exp(s_buf[...] - m)
    l = p_own + jnp.sum(p, axis=1, keepdims=True)

    for j in range(n_pages):
        page_copy(cv_hbm, vbuf, vsem, b, slot, j).wait()
    acc = jnp.zeros((HD, page), F32)
    for j in range(n_pages):
        acc = acc + vbuf[slot, j] * head_bcast(p[:, j * page:(j + 1) * page])
    o_col = jnp.sum(acc, axis=1, keepdims=True) + head_bcast(p_own) * vn_col
    o_col = o_col / head_bcast(l)
    ot_ref[...] = jnp.where(seq_lane, o_col, ot_ref[...])


def _moba_step(qt, knt, vnt, ck, cv, page_table):
    HD, N = qt.shape
    page = ck.shape[2]
    n_pages = page_table.shape[1]
    past_len = n_pages * page
    assert past_len % MOBA_BLOCK == 0 and MOBA_BLOCK % page == 0 and page % LANES == 0
    nb = past_len // MOBA_BLOCK
    assert nb >= 1
    full = pl.BlockSpec((HD, N), lambda b, pt: (0, 0))
    return pl.pallas_call(
        functools.partial(_moba_step_kernel, n_pages=n_pages, page=page, nb=nb,
                          scale=HEAD_DIM ** -0.5),
        grid_spec=pltpu.PrefetchScalarGridSpec(
            num_scalar_prefetch=1,
            grid=(N,),
            in_specs=[full, full, full,
                      pl.BlockSpec(memory_space=pl.ANY),
                      pl.BlockSpec(memory_space=pl.ANY)],
            out_specs=full,
            scratch_shapes=[pltpu.VMEM((2, n_pages, HD, page), F32),
                            pltpu.VMEM((2, n_pages, HD, page), F32),
                            pltpu.VMEM((N_HEADS, past_len), F32),
                            pltpu.SemaphoreType.DMA((2,)),
                            pltpu.SemaphoreType.DMA((2,))]),
        out_shape=jax.ShapeDtypeStruct((HD, N), F32),
        compiler_params=pltpu.CompilerParams(
            dimension_semantics=("arbitrary",), vmem_limit_bytes=VMEM_LIMIT),
        name="moba_step",
    )(page_table, qt, knt, vnt, ck, cv)


def kernel(x_prompt, x_sample, state_conv, state_h, cache_k, cache_v, page_table, norm_mix, norm_mlp, w_ff1, w_ff2, w_rg_in, b_rg_in, conv_w, conv_b, w_gate_a, b_gate_a, w_gate_i, b_gate_i, lru_lambda, w_rg_out, b_rg_out, norm_kv, w_kv, w_q, w_o, norm_out):
    B, T, D = x_prompt.shape
    NS, TS, _ = x_sample.shape
    depth = norm_mix.shape[0]
    assert depth == 2 and w_rg_in.shape[0] == 1 and w_q.shape[0] == 1 and TS == 1
    HD = N_HEADS * HEAD_DIM
    d_rnn = w_rg_in.shape[2] // 2
    hist = CONV_W - 1
    n_phys, page = cache_k.shape[0], cache_k.shape[1]
    bf = lambda w: w.astype(BF16)

    rg = (norm_mix[0], bf(w_rg_in[0]), b_rg_in[0], conv_w[0], conv_b[0],
          bf(w_gate_a[0]), b_gate_a[0].reshape(-1), bf(w_gate_i[0]), b_gate_i[0].reshape(-1),
          lru_lambda[0], bf(w_rg_out[0]), b_rg_out[0])
    mlp0 = (norm_mlp[0], bf(w_ff1[0]), bf(w_ff2[0]), norm_kv, bf(w_kv.T), norm_mix[1], bf(w_q[0]))
    mlp1 = (bf(w_o[0]), norm_mlp[1], bf(w_ff1[1]), bf(w_ff2[1]), norm_out)

    def heads_last(xt):
        lead = xt.shape[:-2]
        xt = xt.reshape(lead + (N_HEADS, HEAD_DIM, xt.shape[-1]))
        return jnp.moveaxis(xt, -1, -3)

    x1, conv_p, h_p = _rglru_seq(x_prompt, jnp.zeros((B, hist, d_rnn), F32),
                                 jnp.zeros((B, d_rnn), F32), *rg, tm=256)
    x2, kt_p, vt_p, q_p = _mlp_kv(x1, B, *mlp0, tm=256)
    attn_p = _moba_seq(q_p, _moba_gate(q_p, kt_p, tq=min(T, 1024)), kt_p, vt_p)
    y_p = _attn_out_mlp(x2, attn_p, *mlp1, tm=256)

    xs = x_sample.reshape(NS, D)
    x1s, conv_s, h_s = _rglru_step(xs, jnp.swapaxes(state_conv[0], 0, 1), state_h[0], *rg)
    x2s, kt_s, vt_s, q_s = _mlp_kv(x1s, 1, *mlp0, tm=NS)
    ck = jnp.transpose(cache_k, (0, 2, 3, 1)).reshape(n_phys, HD, page)
    cv = jnp.transpose(cache_v, (0, 2, 3, 1)).reshape(n_phys, HD, page)
    attn_s = _moba_step(q_s.T, kt_s[0], vt_s[0], ck, cv, page_table).T
    y_s = _attn_out_mlp(x2s, attn_s, *mlp1, tm=NS)

    return (y_p.reshape(B, T, D), y_s.reshape(NS, 1, D),
            conv_p[None], h_p[None], heads_last(kt_p), heads_last(vt_p),
            jnp.swapaxes(conv_s, 0, 1)[None], h_s[None],
            heads_last(kt_s[0])[:, None], heads_last(vt_s[0])[:, None])
```

```python
import functools
import math

import jax
import jax.numpy as jnp
from jax import lax
from jax.experimental import pallas as pl
from jax.experimental.pallas import tpu as pltpu

N_HEADS = 16
HEAD_DIM = 64
N_LRU_BLOCKS = 4
CONV_W = 4
LRU_C = 8.0
MOBA_BLOCK = 256
MOBA_TOPK = 3
EPS = 1e-6
NEG = -1e30
LOG2E = math.log2(math.e)

LANES = 128
SUBLANES = 8
HEADS_PER_GROUP = LANES // HEAD_DIM
VMEM_LIMIT = 56 * 1024 * 1024

BF16 = jnp.bfloat16
F32 = jnp.float32


def _rmsnorm(x, g):
    return x * lax.rsqrt(jnp.mean(x * x, axis=-1, keepdims=True) + EPS) * g


def _rmsnorm_mxu(x, g):
    d = x.shape[1]
    ss = jnp.dot((x * x).astype(BF16), jnp.ones((d, LANES), BF16), preferred_element_type=F32)
    inv = lax.rsqrt(ss * (1.0 / d) + EPS)
    return x * jnp.concatenate([inv] * (d // LANES), axis=1) * g


def _dot(a, b, precision=None):
    return jnp.dot(a, b, preferred_element_type=F32, precision=precision)


def _dot_nt(a, b):
    return lax.dot_general(a, b, (((1,), (1,)), ((), ())), preferred_element_type=F32)


def _gelu_tanh(x):
    c = math.sqrt(2.0 / math.pi)
    return x * (0.5 * (1.0 + jnp.tanh(c * (x + 0.044715 * (x * x * x)))))


def _softplus(y):
    return jnp.maximum(y, 0.0) + jnp.log1p(jnp.exp(-jnp.abs(y)))


def _neg_expm1(x):
    return jnp.tanh(-0.5 * x) * (jnp.exp(x) + 1.0)


def _lru_coeffs(c, wga_ref, bga, wgi_ref, bgi, lam):
    blk = c.shape[1] // N_LRU_BLOCKS
    cb = c.astype(BF16)
    ra, ri = [], []
    for n in range(N_LRU_BLOCKS):
        cn = cb[:, n * blk:(n + 1) * blk]
        ra.append(_dot(cn, wga_ref[n]))
        ri.append(_dot(cn, wgi_ref[n]))
    r = jax.nn.sigmoid(jnp.concatenate(ra, axis=1) + bga)
    i = jax.nn.sigmoid(jnp.concatenate(ri, axis=1) + bgi)
    log_a = -LRU_C * r * _softplus(-lam)
    a = jnp.exp(log_a)
    mult = jnp.sqrt(_neg_expm1(2.0 * log_a))
    return a, mult * (i * c)


def _rglru_seq_kernel(x_ref, conv0_ref, h0_ref, g_ref, win_ref, bin_ref, cw_ref, cb_ref,
                      wga_ref, bga_ref, wgi_ref, bgi_ref, lam_ref, wout_ref, bout_ref,
                      x1_ref, convn_ref, hn_ref,
                      ubuf, aloc, hloc, hcar, *, tm, d_rnn):
    t = pl.program_id(1)
    nt = pl.num_programs(1)
    hist = CONV_W - 1

    @pl.when(t == 0)
    def _():
        ubuf[SUBLANES - hist:SUBLANES, :] = conv0_ref[0]
        hcar[...] = h0_ref[0]

    x = x_ref[...]
    xn = _rmsnorm_mxu(x, g_ref[...]).astype(BF16)
    proj = _dot(xn, win_ref[...]) + bin_ref[...]
    gate = _gelu_tanh(proj[:, :d_rnn])
    u = proj[:, d_rnn:]
    ubuf[SUBLANES:SUBLANES + tm, :] = u

    c = cb_ref[...] + ubuf[SUBLANES - hist:SUBLANES - hist + tm, :] * cw_ref[0:1, :]
    for k in range(1, CONV_W):
        off = SUBLANES - hist + k
        c = c + ubuf[off:off + tm, :] * cw_ref[k:k + 1, :]

    a, b = _lru_coeffs(c, wga_ref, bga_ref[...], wgi_ref, bgi_ref[...], lam_ref[...])

    a = a.reshape(tm // SUBLANES, SUBLANES, d_rnn)
    b = b.reshape(tm // SUBLANES, SUBLANES, d_rnn)
    row = lax.broadcasted_iota(jnp.int32, a.shape, 1)
    for s in (1, 2, 4):
        ok = row >= s
        a_sh = pltpu.roll(a, s, 1)
        b_sh = pltpu.roll(b, s, 1)
        b = jnp.where(ok, a * b_sh + b, b)
        a = jnp.where(ok, a * a_sh, a)
    aloc[...] = a.reshape(tm, d_rnn)
    hloc[...] = b.reshape(tm, d_rnn)

    def body(gi, carry):
        r0 = pl.multiple_of(gi * SUBLANES, SUBLANES)
        h = hloc[pl.ds(r0, SUBLANES), :] + aloc[pl.ds(r0, SUBLANES), :] * carry
        hloc[pl.ds(r0, SUBLANES), :] = h
        return h[SUBLANES - 1:SUBLANES, :]

    hcar[...] = lax.fori_loop(0, tm // SUBLANES, body, hcar[...])

    hg = (hloc[...] * gate).astype(BF16)
    x1_ref[...] = x + _dot(hg, wout_ref[...]) + bout_ref[...]

    ubuf[SUBLANES - hist:SUBLANES, :] = ubuf[SUBLANES + tm - hist:SUBLANES + tm, :]

    @pl.when(t == nt - 1)
    def _():
        convn_ref[0] = ubuf[SUBLANES - hist:SUBLANES, :]
        hn_ref[0] = hcar[...]


def _rglru_seq(x, conv0, h0, g, w_in, b_in, cw, cb, wga, bga, wgi, bgi, lam, w_out, b_out, *, tm):
    B, T, D = x.shape
    d_rnn = w_in.shape[1] // 2
    hist = CONV_W - 1
    assert T % tm == 0 and tm % SUBLANES == 0 and tm >= hist
    nt = T // tm
    x2 = x.reshape(B * T, D)
    full = lambda a: pl.BlockSpec(a.shape, lambda b, t: (0,) * a.ndim)
    row = lambda v: v.reshape(1, -1)
    args = (x2, conv0, h0.reshape(B, 1, d_rnn), row(g), w_in, row(b_in), cw, row(cb),
            wga, row(bga), wgi, row(bgi), row(lam), w_out, row(b_out))
    in_specs = [pl.BlockSpec((tm, D), lambda b, t: (b * nt + t, 0)),
                pl.BlockSpec((1, hist, d_rnn), lambda b, t: (b, 0, 0)),
                pl.BlockSpec((1, 1, d_rnn), lambda b, t: (b, 0, 0))]
    in_specs += [full(a) for a in args[3:]]
    x1, convn, hn = pl.pallas_call(
        functools.partial(_rglru_seq_kernel, tm=tm, d_rnn=d_rnn),
        grid=(B, nt),
        in_specs=in_specs,
        out_specs=[pl.BlockSpec((tm, D), lambda b, t: (b * nt + t, 0)),
                   pl.BlockSpec((1, hist, d_rnn), lambda b, t: (b, 0, 0)),
                   pl.BlockSpec((1, 1, d_rnn), lambda b, t: (b, 0, 0))],
        out_shape=[jax.ShapeDtypeStruct((B * T, D), F32),
                   jax.ShapeDtypeStruct((B, hist, d_rnn), F32),
                   jax.ShapeDtypeStruct((B, 1, d_rnn), F32)],
        scratch_shapes=[pltpu.VMEM((tm + SUBLANES, d_rnn), F32),
                        pltpu.VMEM((tm, d_rnn), F32),
                        pltpu.VMEM((tm, d_rnn), F32),
                        pltpu.VMEM((1, d_rnn), F32)],
        compiler_params=pltpu.CompilerParams(
            dimension_semantics=("arbitrary", "arbitrary"), vmem_limit_bytes=VMEM_LIMIT),
        name="rglru_seq",
    )(*args)
    return x1, convn, hn.reshape(B, d_rnn)


def _rglru_step_kernel(x_ref, conv_ref, h0_ref, g_ref, win_ref, bin_ref, cw_ref, cb_ref,
                       wga_ref, bga_ref, wgi_ref, bgi_ref, lam_ref, wout_ref, bout_ref,
                       x1_ref, convn_ref, hn_ref, *, d_rnn):
    hist = CONV_W - 1
    x = x_ref[...]
    xn = _rmsnorm(x, g_ref[...]).astype(BF16)
    proj = _dot(xn, win_ref[...]) + bin_ref[...]
    gate = _gelu_tanh(proj[:, :d_rnn])
    u = proj[:, d_rnn:]
    c = cb_ref[...] + conv_ref[0] * cw_ref[0:1, :]
    for k in range(1, hist):
        c = c + conv_ref[k] * cw_ref[k:k + 1, :]
    c = c + u * cw_ref[hist:hist + 1, :]
    a, b = _lru_coeffs(c, wga_ref, bga_ref[...], wgi_ref, bgi_ref[...], lam_ref[...])
    h = a * h0_ref[...] + b
    hn_ref[...] = h
    for k in range(hist - 1):
        convn_ref[k] = conv_ref[k + 1]
    convn_ref[hist - 1] = u
    x1_ref[...] = x + _dot((h * gate).astype(BF16), wout_ref[...]) + bout_ref[...]


def _rglru_step(x, conv0, h0, g, w_in, b_in, cw, cb, wga, bga, wgi, bgi, lam, w_out, b_out):
    N, D = x.shape
    d_rnn = w_in.shape[1] // 2
    hist = CONV_W - 1
    row = lambda v: v.reshape(1, -1)
    args = (x, conv0, h0, row(g), w_in, row(b_in), cw, row(cb),
            wga, row(bga), wgi, row(bgi), row(lam), w_out, row(b_out))
    return pl.pallas_call(
        functools.partial(_rglru_step_kernel, d_rnn=d_rnn),
        out_shape=[jax.ShapeDtypeStruct((N, D), F32),
                   jax.ShapeDtypeStruct((hist, N, d_rnn), F32),
                   jax.ShapeDtypeStruct((N, d_rnn), F32)],
        compiler_params=pltpu.CompilerParams(vmem_limit_bytes=VMEM_LIMIT),
        name="rglru_step",
    )(*args)


def _sqrelu_mlp(xn_bf, w1_ref, w2_ref, fc, lo=0, hi=None):
    hi = w1_ref.shape[1] if hi is None else hi
    acc = None
    for c0 in range(lo, hi, fc):
        h = jnp.maximum(_dot(xn_bf, w1_ref[:, c0:c0 + fc]), 0.0)
        part = _dot((h * h).astype(BF16), w2_ref[c0:c0 + fc, :])
        acc = part if acc is None else acc + part
    return acc


def _mlp_kv_kernel(x_ref, gm_ref, w1_ref, w2_ref, gkv_ref, wkvt_ref, gq_ref, wq_ref,
                   x2_ref, kt_ref, vt_ref, q_ref, *, fc):
    x = x_ref[...]
    x2 = x + _sqrelu_mlp(_rmsnorm(x, gm_ref[...]).astype(BF16), w1_ref, w2_ref, fc)
    x2_ref[...] = x2
    hd = kt_ref.shape[1]
    kvt = _dot_nt(wkvt_ref[...], _rmsnorm(x2, gkv_ref[...]).astype(BF16))
    kt_ref[0] = kvt[:hd, :]
    vt_ref[0] = kvt[hd:, :]
    q_ref[...] = _dot(_rmsnorm(x2, gq_ref[...]).astype(BF16), wq_ref[...])


def _mlp_kv(x, B, gm, w1, w2, gkv, wkvt, gq, wq, *, tm, fc=512):
    N, D = x.shape
    hd = wq.shape[1]
    T = N // B
    assert N == B * T and T % tm == 0
    nt = T // tm
    row = lambda v: v.reshape(1, -1)
    args = (x, row(gm), w1, w2, row(gkv), wkvt, row(gq), wq)
    full = lambda a: pl.BlockSpec(a.shape, lambda i: (0,) * a.ndim)
    tile = lambda w: pl.BlockSpec((tm, w), lambda i: (i, 0))
    tile_t = pl.BlockSpec((1, hd, tm), lambda i: (i // nt, 0, i % nt))
    return pl.pallas_call(
        functools.partial(_mlp_kv_kernel, fc=fc),
        grid=(N // tm,),
        in_specs=[tile(D)] + [full(a) for a in args[1:]],
        out_specs=[tile(D), tile_t, tile_t, tile(hd)],
        out_shape=[jax.ShapeDtypeStruct((N, D), F32),
                   jax.ShapeDtypeStruct((B, hd, T), F32),
                   jax.ShapeDtypeStruct((B, hd, T), F32),
                   jax.ShapeDtypeStruct((N, hd), F32)],
        compiler_params=pltpu.CompilerParams(
            dimension_semantics=("arbitrary",), vmem_limit_bytes=VMEM_LIMIT),
        name="mlp_kv",
    )(*args)


def _attn_out_mlp_kernel(x_ref, a_ref, wo_ref, gm_ref, w1_ref, w2_ref, go_ref, y_ref, *, fc):
    x3 = x_ref[...] + _dot(a_ref[...].astype(BF16), wo_ref[...])
    x4 = x3 + _sqrelu_mlp(_rmsnorm(x3, gm_ref[...]).astype(BF16), w1_ref, w2_ref, fc)
    y_ref[...] = _rmsnorm(x4, go_ref[...])


def _attn_out_mlp(x, attn, wo, gm, w1, w2, go, *, tm, fc=512):
    N, D = x.shape
    assert N % tm == 0
    row = lambda v: v.reshape(1, -1)
    args = (x, attn, wo, row(gm), w1, w2, row(go))
    full = lambda a: pl.BlockSpec(a.shape, lambda i: (0,) * a.ndim)
    tile = lambda w: pl.BlockSpec((tm, w), lambda i: (i, 0))
    return pl.pallas_call(
        functools.partial(_attn_out_mlp_kernel, fc=fc),
        grid=(N // tm,),
        in_specs=[tile(D), tile(attn.shape[1])] + [full(a) for a in args[2:]],
        out_specs=tile(D),
        out_shape=jax.ShapeDtypeStruct((N, D), F32),
        compiler_params=pltpu.CompilerParams(
            dimension_semantics=("arbitrary",), vmem_limit_bytes=VMEM_LIMIT),
        name="attn_out_mlp",
    )(*args)


def _topk_select(gate, allowed, axis=1):
    nb = gate.shape[axis]
    blk = lax.broadcasted_iota(jnp.int32, gate.shape, axis)
    g = jnp.where(allowed, gate, NEG)
    sel = jnp.zeros(gate.shape, jnp.bool_)
    for _ in range(MOBA_TOPK):
        m = jnp.max(g, axis=axis, keepdims=True)
        first = jnp.min(jnp.where(g == m, blk, nb), axis=axis, keepdims=True)
        pick = blk == first
        sel = jnp.logical_or(sel, pick)
        g = jnp.where(pick, -jnp.inf, g)
    return jnp.logical_and(sel, allowed)


def _block_sums(x, nb):
    blk = lax.broadcasted_iota(jnp.int32, (x.shape[0], nb), 1)
    out = jnp.zeros((x.shape[0], nb), F32)
    for j in range(nb):
        sj = jnp.sum(x[:, j * MOBA_BLOCK:(j + 1) * MOBA_BLOCK], axis=1, keepdims=True)
        out = jnp.where(blk == j, sj, out)
    return out


def _moba_gate_kernel(q_ref, kt_ref, bias_ref, kw, *, nb, nbp, tq):
    t = pl.program_id(2)
    blk_rows = MOBA_BLOCK

    @pl.when(t == 0)
    def _():
        km_t = _block_sums(kt_ref[0], nb) * (1.0 / blk_rows)
        km_t = jnp.concatenate([km_t, jnp.zeros((LANES, LANES - nb), F32)], axis=1)
        km = km_t.T[0:nbp, :]
        feat = lax.broadcasted_iota(jnp.int32, km.shape, 1)
        for hh in range(HEADS_PER_GROUP):
            own = jnp.logical_and(feat >= hh * HEAD_DIM, feat < (hh + 1) * HEAD_DIM)
            k_h = jnp.where(own, km, 0.0)
            k_hi = k_h.astype(BF16)
            k_lo = (k_h - k_hi.astype(F32)).astype(BF16)
            kw[hh, 0:nbp, :] = jnp.concatenate([k_hi, k_hi], axis=1)
            kw[hh, nbp:, :] = jnp.concatenate([k_lo, jnp.zeros_like(k_lo)], axis=1)

    q2 = q_ref[...]
    q_hi = q2.astype(BF16)
    q_lo = (q2 - q_hi.astype(F32)).astype(BF16)
    q_cat = jnp.concatenate([q_hi, q_lo], axis=1)
    blk_t = lax.broadcasted_iota(jnp.int32, (nbp, tq), 0)
    q_blk = (t * tq + lax.broadcasted_iota(jnp.int32, (nbp, tq), 1)) // blk_rows
    for hh in range(HEADS_PER_GROUP):
        parts = _dot_nt(kw[hh], q_cat)
        gate_t = parts[0:nbp, :] + parts[nbp:, :]
        sel_t = _topk_select(gate_t, blk_t < q_blk, axis=0)
        bias_t = jnp.where(jnp.logical_or(sel_t, blk_t == q_blk), 0.0, NEG)
        bias_t = jnp.concatenate([bias_t, jnp.zeros((LANES - nbp, tq), F32)], axis=0)
        bias_ref[hh] = bias_t.T.astype(BF16)


def _moba_gate(q, kt, *, tq=1024):
    N, HD = q.shape
    B, _, T = kt.shape
    assert N == B * T and T % tq == 0 and tq % MOBA_BLOCK == 0 and HD % LANES == 0
    nb = T // MOBA_BLOCK
    nbp = -(-nb // (2 * SUBLANES)) * (2 * SUBLANES)
    assert nbp <= LANES
    nt = T // tq
    return pl.pallas_call(
        functools.partial(_moba_gate_kernel, nb=nb, nbp=nbp, tq=tq),
        grid=(B, HD // LANES, nt),
        in_specs=[pl.BlockSpec((tq, LANES), lambda b, g, t: (b * nt + t, g)),
                  pl.BlockSpec((1, LANES, T), lambda b, g, t: (b, g, 0))],
        out_specs=pl.BlockSpec((HEADS_PER_GROUP, tq, LANES), lambda b, g, t: (g, b * nt + t, 0)),
        out_shape=jax.ShapeDtypeStruct((HD // HEAD_DIM, N, LANES), BF16),
        scratch_shapes=[pltpu.VMEM((HEADS_PER_GROUP, 2 * nbp, 2 * LANES), BF16)],
        compiler_params=pltpu.CompilerParams(
            dimension_semantics=("arbitrary", "arbitrary", "arbitrary"),
            vmem_limit_bytes=VMEM_LIMIT),
        name="moba_gate",
    )(q, kt)


def _moba_seq_kernel(q_ref, bias_ref, kt_ref, vt_ref, o_ref, ktaug, vtaug, sbuf_even, sbuf_odd, msbuf, *,
                     nb, bpc, scale):
    i = pl.program_id(2)
    blk_rows = MOBA_BLOCK
    kc = bpc * blk_rows
    nchunk = ktaug.shape[0]
    c_own = i // bpc
    heads = range(HEADS_PER_GROUP)

    @pl.when(i == 0)
    def _():
        key_blk = lax.broadcasted_iota(jnp.int32, (kc, LANES), 0) // blk_rows
        blk_lane = lax.broadcasted_iota(jnp.int32, (kc, LANES), 1)
        feat_row = lax.broadcasted_iota(jnp.int32, (LANES, kc), 0)
        for c in range(nchunk):
            ktaug[c, :, 0:LANES] = kt_ref[0, :, c * kc:(c + 1) * kc].T.astype(BF16)
            ktaug[c, :, LANES:] = jnp.where(blk_lane == key_blk + c * bpc, 1.0, 0.0).astype(BF16)
            vt_c = vt_ref[0, :, c * kc:(c + 1) * kc]
            for hh in heads:
                own = jnp.logical_and(feat_row >= hh * HEAD_DIM, feat_row < (hh + 1) * HEAD_DIM)
                ones_row = ((hh + 1) % HEADS_PER_GROUP) * HEAD_DIM
                vtaug[hh, c] = jnp.where(own, vt_c, jnp.where(feat_row == ones_row, 1.0, 0.0)).astype(BF16)

    def group_max(s, m):
        sm = jnp.max(s.reshape(kc // SUBLANES, SUBLANES, blk_rows), axis=0)
        return sm if m is None else jnp.maximum(m, sm)

    def pairwise(n, body, carry):
        carry = lax.fori_loop(0, n // 2, lambda t, cr: body(2 * t + 1, body(2 * t, cr)), carry)
        return lax.fori_loop(0, n % 2, lambda _, cr: body(n - 1, cr), carry)

    def make_q_augs():
        q2 = q_ref[...]
        lane = lax.broadcasted_iota(jnp.int32, q2.shape, 1)
        out = []
        for hh in heads:
            in_head = jnp.logical_and(lane >= hh * HEAD_DIM, lane < (hh + 1) * HEAD_DIM)
            qs = (jnp.where(in_head, q2, 0.0) * (scale * LOG2E)).astype(BF16)
            out.append(jnp.concatenate([qs, bias_ref[hh]], axis=1))
        return out

    def own_chunk_scores(s_cur, q_augs):
        k_id = lax.broadcasted_iota(jnp.int32, (kc, blk_rows), 0)
        q_id = lax.broadcasted_iota(jnp.int32, (kc, blk_rows), 1)
        own_off = (i - c_own * bpc) * blk_rows
        keep = jnp.logical_or(k_id - own_off <= q_id,
                              jnp.logical_or(k_id < own_off, k_id >= own_off + blk_rows))
        mruns = []
        for hh in heads:
            s = jnp.where(keep, _dot_nt(ktaug[c_own], q_augs[hh]), NEG)
            s_cur[hh, c_own] = s
            mruns.append(group_max(s, None))
        return tuple(mruns)

    def score_chunk(s_cur, c, q_augs, mruns):
        out = []
        for hh in heads:
            s = _dot_nt(ktaug[c], q_augs[hh])
            s_cur[hh, c] = s
            out.append(group_max(s, mruns[hh]))
        return tuple(out)

    def pv_chunk(s_prv, c, ms, accs):
        return tuple(accs[hh] + _dot(vtaug[hh, c], jnp.exp2(s_prv[hh, c] - ms[hh]).astype(BF16))
                     for hh in heads)

    def store_max(cur, mruns):
        for hh in heads:
            msbuf[cur, hh] = jnp.max(mruns[hh], axis=0, keepdims=True)

    def store_out(accs):
        feat = lax.broadcasted_iota(jnp.int32, (LANES, blk_rows), 0)
        o_t = None
        for hh in heads:
            ones_row = ((hh + 1) % HEADS_PER_GROUP) * HEAD_DIM
            oh = accs[hh] / accs[hh][ones_row:ones_row + 1, :]
            o_t = oh if o_t is None else jnp.where(feat >= hh * HEAD_DIM, oh, o_t)
        o_ref[...] = o_t.T

    zero_acc = (jnp.zeros((LANES, blk_rows), F32),) * HEADS_PER_GROUP

    def step(cur, s_cur, s_prv):
        prv = 1 - cur
        parity = i % 2 == cur

        @pl.when(jnp.logical_and(parity, i == 0))
        def _():
            store_max(cur, own_chunk_scores(s_cur, make_q_augs()))
            o_ref[...] = jnp.zeros(o_ref.shape, F32)

        @pl.when(jnp.logical_and(parity, jnp.logical_and(i > 0, i < nb)))
        def _():
            q_augs = make_q_augs()
            ms = [msbuf[prv, hh] for hh in heads]
            c_last = (i - 1) // bpc
            mruns = own_chunk_scores(s_cur, q_augs)
            accs = pv_chunk(s_prv, c_last, ms, zero_acc)

            def both(c, carry):
                mruns, accs = carry
                return score_chunk(s_cur, c, q_augs, mruns), pv_chunk(s_prv, c, ms, accs)

            mruns, accs = pairwise(c_last, both, (mruns, accs))
            n_extra = jnp.where(i % bpc == 0, 1, 0)
            mruns = lax.fori_loop(0, n_extra, lambda _, m: score_chunk(s_cur, c_last, q_augs, m), mruns)
            store_max(cur, mruns)
            store_out(accs)

        @pl.when(jnp.logical_and(parity, i == nb))
        def _():
            ms = [msbuf[prv, hh] for hh in heads]
            store_out(pairwise(nchunk, lambda c, a: pv_chunk(s_prv, c, ms, a), zero_acc))

    step(0, sbuf_even, sbuf_odd)
    step(1, sbuf_odd, sbuf_even)


def _moba_seq(q, bias, kt, vt, *, bpc=4):
    N, HD = q.shape
    B, _, T = kt.shape
    assert N == B * T and T % (bpc * MOBA_BLOCK) == 0 and HD % LANES == 0
    assert HEADS_PER_GROUP >= 2
    nb = T // MOBA_BLOCK
    nchunk = nb // bpc
    kc = bpc * MOBA_BLOCK
    q_blk = lambda b, g, i: (b * nb + jnp.minimum(i, nb - 1), g)
    o_blk = lambda b, g, i: (b * nb + jnp.maximum(i - 1, 0), g)
    resident = dict(pipeline_mode=pl.Buffered(1))
    return pl.pallas_call(
        functools.partial(_moba_seq_kernel, nb=nb, bpc=bpc, scale=HEAD_DIM ** -0.5),
        grid=(B, HD // LANES, nb + 1),
        in_specs=[pl.BlockSpec((MOBA_BLOCK, LANES), q_blk),
                  pl.BlockSpec((HEADS_PER_GROUP, MOBA_BLOCK, LANES),
                               lambda b, g, i: (g, b * nb + jnp.minimum(i, nb - 1), 0)),
                  pl.BlockSpec((1, LANES, T), lambda b, g, i: (b, g, 0), **resident),
                  pl.BlockSpec((1, LANES, T), lambda b, g, i: (b, g, 0), **resident)],
        out_specs=pl.BlockSpec((MOBA_BLOCK, LANES), o_blk),
        out_shape=jax.ShapeDtypeStruct((N, HD), F32),
        scratch_shapes=[pltpu.VMEM((nchunk, kc, 2 * LANES), BF16),
                        pltpu.VMEM((HEADS_PER_GROUP, nchunk, LANES, kc), BF16),
                        pltpu.VMEM((HEADS_PER_GROUP, nchunk, kc, MOBA_BLOCK), F32),
                        pltpu.VMEM((HEADS_PER_GROUP, nchunk, kc, MOBA_BLOCK), F32),
                        pltpu.VMEM((2, HEADS_PER_GROUP, 1, MOBA_BLOCK), F32)],
        compiler_params=pltpu.CompilerParams(
            dimension_semantics=("arbitrary", "arbitrary", "arbitrary"),
            vmem_limit_bytes=VMEM_LIMIT),
        name="moba_seq",
    )(q, bias, kt, vt)


def _tail_paged_kernel(pt_ref, x_ref, a_ref, wo_ref, gm_ref, w1_ref, w2_ref, go_ref,
                       qt_ref, knt_ref, vnt_ref, ck_hbm, cv_hbm, y_ref, ot_ref,
                       kbuf, vbuf, s_buf, ksem, vsem, *, fc, n_pages, page, nb, scale):
    b = pl.program_id(0)
    last = pl.num_programs(0) - 1
    HD = qt_ref.shape[0]
    d_ff = w1_ref.shape[1]

    def page_copy(src_hbm, dst, sem, seq, j):
        return pltpu.make_async_copy(src_hbm.at[pt_ref[seq, j]], dst.at[j], sem.at[0])

    def start_pages(src_hbm, dst, sem, seq):
        for j in range(n_pages):
            page_copy(src_hbm, dst, sem, seq, j).start()

    def wait_pages(src_hbm, dst, sem):
        for j in range(n_pages):
            page_copy(src_hbm, dst, sem, b, j).wait()

    def head_sums(x):
        return jnp.sum(x.reshape(N_HEADS, HEAD_DIM, x.shape[1]), axis=1)

    def head_bcast(x):
        return jnp.broadcast_to(x[:, None, :], (N_HEADS, HEAD_DIM, x.shape[1])).reshape(HD, x.shape[1])

    @pl.when(b == 0)
    def _():
        start_pages(ck_hbm, kbuf, ksem, b)
        start_pages(cv_hbm, vbuf, vsem, b)
        ot_ref[...] = jnp.zeros(ot_ref.shape, F32)

    seq_lane = lax.broadcasted_iota(jnp.int32, qt_ref.shape, 1) == b
    column = lambda ref: jnp.sum(jnp.where(seq_lane, ref[...], 0.0), axis=1, keepdims=True)
    q_col = column(qt_ref) * scale
    kn_col = column(knt_ref)
    vn_col = column(vnt_ref)
    q_b = jnp.broadcast_to(q_col, (HD, page))

    nxt = jnp.minimum(b + 1, last)
    wait_pages(ck_hbm, kbuf, ksem)
    for j in range(n_pages):
        s_buf[:, j * page:(j + 1) * page] = head_sums(kbuf[j] * q_b)
    x3 = x_ref[...] + _dot(a_ref[...].astype(BF16), wo_ref[...])
    xn = _rmsnorm(x3, gm_ref[...]).astype(BF16)
    half = (d_ff // fc // 2) * fc
    mlp = _sqrelu_mlp(xn, w1_ref, w2_ref, fc, 0, half)
    start_pages(ck_hbm, kbuf, ksem, nxt)

    gate = _block_sums(s_buf[...], nb) * (1.0 / MOBA_BLOCK)
    sel = _topk_select(gate, jnp.ones(gate.shape, jnp.bool_))
    s_own = head_sums(q_col * kn_col)
    m = s_own
    for j in range(nb):
        sj = jnp.where(sel[:, j:j + 1], s_buf[:, j * MOBA_BLOCK:(j + 1) * MOBA_BLOCK], NEG)
        s_buf[:, j * MOBA_BLOCK:(j + 1) * MOBA_BLOCK] = sj
        m = jnp.maximum(m, jnp.max(sj, axis=1, keepdims=True))
    p_own = jnp.exp(s_own - m)
    p = jnp.exp(s_buf[...] - m)
    l = p_own + jnp.sum(p, axis=1, keepdims=True)

    wait_pages(cv_hbm, vbuf, vsem)
    acc = jnp.zeros((HD, page), F32)
    for j in range(n_pages):
        acc = acc + vbuf[j] * head_bcast(p[:, j * page:(j + 1) * page])
    x4 = x3 + mlp + _sqrelu_mlp(xn, w1_ref, w2_ref, fc, half, d_ff)
    y_ref[...] = _rmsnorm(x4, go_ref[...])
    start_pages(cv_hbm, vbuf, vsem, nxt)
    o_col = jnp.sum(acc, axis=1, keepdims=True) + head_bcast(p_own) * vn_col
    o_col = o_col / head_bcast(l)
    ot_ref[...] = jnp.where(seq_lane, o_col, ot_ref[...])

    @pl.when(b == last)
    def _():
        wait_pages(ck_hbm, kbuf, ksem)
        wait_pages(cv_hbm, vbuf, vsem)


def _tail_paged(x, attn, wo, gm, w1, w2, go, qt, knt, vnt, ck, cv, page_table, *, fc=512):
    N, D = x.shape
    HD, NS = qt.shape
    page = ck.shape[2]
    n_pages = page_table.shape[1]
    past_len = n_pages * page
    assert past_len % MOBA_BLOCK == 0 and MOBA_BLOCK % page == 0 and page % LANES == 0
    nb = past_len // MOBA_BLOCK
    assert nb >= 1 and N % NS == 0 and (N // NS) % SUBLANES == 0
    tm = N // NS
    row = lambda v: v.reshape(1, -1)
    weights = (wo, row(gm), w1, w2, row(go))
    full = lambda a: pl.BlockSpec(a.shape, lambda b, pt: (0,) * a.ndim)
    tile = lambda w: pl.BlockSpec((tm, w), lambda b, pt: (b, 0))
    cols = pl.BlockSpec((HD, NS), lambda b, pt: (0, 0))
    return pl.pallas_call(
        functools.partial(_tail_paged_kernel, fc=fc, n_pages=n_pages, page=page, nb=nb,
                          scale=HEAD_DIM ** -0.5),
        grid_spec=pltpu.PrefetchScalarGridSpec(
            num_scalar_prefetch=1,
            grid=(NS,),
            in_specs=[tile(D), tile(attn.shape[1])] + [full(a) for a in weights] + [cols, cols, cols,
                      pl.BlockSpec(memory_space=pl.ANY),
                      pl.BlockSpec(memory_space=pl.ANY)],
            out_specs=[tile(D), cols],
            scratch_shapes=[pltpu.VMEM((n_pages, HD, page), F32),
                            pltpu.VMEM((n_pages, HD, page), F32),
                            pltpu.VMEM((N_HEADS, past_len), F32),
                            pltpu.SemaphoreType.DMA((1,)),
                            pltpu.SemaphoreType.DMA((1,))]),
        out_shape=[jax.ShapeDtypeStruct((N, D), F32), jax.ShapeDtypeStruct((HD, NS), F32)],
        compiler_params=pltpu.CompilerParams(
            dimension_semantics=("arbitrary",), vmem_limit_bytes=VMEM_LIMIT),
        name="tail_paged",
    )(page_table, x, attn, *weights, qt, knt, vnt, ck, cv)


def kernel(x_prompt, x_sample, state_conv, state_h, cache_k, cache_v, page_table, norm_mix, norm_mlp, w_ff1, w_ff2, w_rg_in, b_rg_in, conv_w, conv_b, w_gate_a, b_gate_a, w_gate_i, b_gate_i, lru_lambda, w_rg_out, b_rg_out, norm_kv, w_kv, w_q, w_o, norm_out):
    B, T, D = x_prompt.shape
    NS, TS, _ = x_sample.shape
    depth = norm_mix.shape[0]
    assert depth == 2 and w_rg_in.shape[0] == 1 and w_q.shape[0] == 1 and TS == 1
    HD = N_HEADS * HEAD_DIM
    d_rnn = w_rg_in.shape[2] // 2
    hist = CONV_W - 1
    n_phys, page = cache_k.shape[0], cache_k.shape[1]
    bf = lambda w: w.astype(BF16)

    rg = (norm_mix[0], bf(w_rg_in[0]), b_rg_in[0], conv_w[0], conv_b[0],
          bf(w_gate_a[0]), b_gate_a[0].reshape(-1), bf(w_gate_i[0]), b_gate_i[0].reshape(-1),
          lru_lambda[0], bf(w_rg_out[0]), b_rg_out[0])
    mlp0 = (norm_mlp[0], bf(w_ff1[0]), bf(w_ff2[0]), norm_kv, bf(w_kv.T), norm_mix[1], bf(w_q[0]))
    mlp1 = (bf(w_o[0]), norm_mlp[1], bf(w_ff1[1]), bf(w_ff2[1]), norm_out)

    def heads_last(xt):
        lead = xt.shape[:-2]
        xt = xt.reshape(lead + (N_HEADS, HEAD_DIM, xt.shape[-1]))
        return jnp.moveaxis(xt, -1, -3)

    x1, conv_p, h_p = _rglru_seq(x_prompt, jnp.zeros((B, hist, d_rnn), F32),
                                 jnp.zeros((B, d_rnn), F32), *rg, tm=256)
    x2, kt_p, vt_p, q_p = _mlp_kv(x1, B, *mlp0, tm=256)
    attn_p = _moba_seq(q_p, _moba_gate(q_p, kt_p, tq=min(T, 1024)), kt_p, vt_p)

    xs = x_sample.reshape(NS, D)
    x1s, conv_s, h_s = _rglru_step(xs, jnp.swapaxes(state_conv[0], 0, 1), state_h[0], *rg)
    x2s, kt_s, vt_s, q_s = _mlp_kv(x1s, 1, *mlp0, tm=NS)
    ck = jnp.transpose(cache_k, (0, 2, 3, 1)).reshape(n_phys, HD, page)
    cv = jnp.transpose(cache_v, (0, 2, 3, 1)).reshape(n_phys, HD, page)
    y_p, attn_st = _tail_paged(x2, attn_p, *mlp1, q_s.T, kt_s[0], vt_s[0], ck, cv, page_table)
    y_s = _attn_out_mlp(x2s, attn_st.T, *mlp1, tm=NS)

    return (y_p.reshape(B, T, D), y_s.reshape(NS, 1, D),
            conv_p[None], h_p[None], heads_last(kt_p), heads_last(vt_p),
            jnp.swapaxes(conv_s, 0, 1)[None], h_s[None],
            heads_last(kt_s[0])[:, None], heads_last(vt_s[0])[:, None])
```

```python
import functools
import math

import jax
import jax.numpy as jnp
from jax import lax
from jax.experimental import pallas as pl
from jax.experimental.pallas import tpu as pltpu

N_HEADS = 16
HEAD_DIM = 64
N_LRU_BLOCKS = 4
CONV_W = 4
LRU_C = 8.0
MOBA_BLOCK = 256
MOBA_TOPK = 3
EPS = 1e-6
NEG = -1e30
LOG2E = math.log2(math.e)

LANES = 128
SUBLANES = 8
HEADS_PER_GROUP = LANES // HEAD_DIM
VMEM_LIMIT = 56 * 1024 * 1024

BF16 = jnp.bfloat16
F32 = jnp.float32


def _rmsnorm(x, g):
    return x * lax.rsqrt(jnp.mean(x * x, axis=-1, keepdims=True) + EPS) * g


def _rmsnorm_mxu(x, g):
    d = x.shape[1]
    ss = jnp.dot((x * x).astype(BF16), jnp.ones((d, LANES), BF16), preferred_element_type=F32)
    inv = lax.rsqrt(ss * (1.0 / d) + EPS)
    return x * jnp.concatenate([inv] * (d // LANES), axis=1) * g


def _dot(a, b, precision=None):
    return jnp.dot(a, b, preferred_element_type=F32, precision=precision)


def _dot_nt(a, b):
    return lax.dot_general(a, b, (((1,), (1,)), ((), ())), preferred_element_type=F32)


def _gelu_tanh(x):
    c = math.sqrt(2.0 / math.pi)
    return x * (0.5 * (1.0 + jnp.tanh(c * (x + 0.044715 * (x * x * x)))))


def _softplus(y):
    return jnp.maximum(y, 0.0) + jnp.log1p(jnp.exp(-jnp.abs(y)))


def _neg_expm1(x):
    return jnp.tanh(-0.5 * x) * (jnp.exp(x) + 1.0)


def _lru_coeffs(c, wga_ref, bga, wgi_ref, bgi, lam):
    blk = c.shape[1] // N_LRU_BLOCKS
    cb = c.astype(BF16)
    ra, ri = [], []
    for n in range(N_LRU_BLOCKS):
        cn = cb[:, n * blk:(n + 1) * blk]
        ra.append(_dot(cn, wga_ref[n]))
        ri.append(_dot(cn, wgi_ref[n]))
    r = jax.nn.sigmoid(jnp.concatenate(ra, axis=1) + bga)
    i = jax.nn.sigmoid(jnp.concatenate(ri, axis=1) + bgi)
    log_a = -LRU_C * r * _softplus(-lam)
    a = jnp.exp(log_a)
    mult = jnp.sqrt(_neg_expm1(2.0 * log_a))
    return a, mult * (i * c)


def _rglru_seq_kernel(x_ref, conv0_ref, h0_ref, g_ref, win_ref, bin_ref, cw_ref, cb_ref,
                      wga_ref, bga_ref, wgi_ref, bgi_ref, lam_ref, wout_ref, bout_ref,
                      x1_ref, convn_ref, hn_ref,
                      ubuf, aloc, hloc, hcar, *, tm, d_rnn):
    t = pl.program_id(1)
    nt = pl.num_programs(1)
    hist = CONV_W - 1

    @pl.when(t == 0)
    def _():
        ubuf[SUBLANES - hist:SUBLANES, :] = conv0_ref[0]
        hcar[...] = h0_ref[0]

    x = x_ref[...]
    xn = _rmsnorm_mxu(x, g_ref[...]).astype(BF16)
    proj = _dot(xn, win_ref[...]) + bin_ref[...]
    gate = _gelu_tanh(proj[:, :d_rnn])
    u = proj[:, d_rnn:]
    ubuf[SUBLANES:SUBLANES + tm, :] = u

    c = cb_ref[...] + ubuf[SUBLANES - hist:SUBLANES - hist + tm, :] * cw_ref[0:1, :]
    for k in range(1, CONV_W):
        off = SUBLANES - hist + k
        c = c + ubuf[off:off + tm, :] * cw_ref[k:k + 1, :]

    a, b = _lru_coeffs(c, wga_ref, bga_ref[...], wgi_ref, bgi_ref[...], lam_ref[...])

    a = a.reshape(tm // SUBLANES, SUBLANES, d_rnn)
    b = b.reshape(tm // SUBLANES, SUBLANES, d_rnn)
    row = lax.broadcasted_iota(jnp.int32, a.shape, 1)
    for s in (1, 2, 4):
        ok = row >= s
        a_sh = pltpu.roll(a, s, 1)
        b_sh = pltpu.roll(b, s, 1)
        b = jnp.where(ok, a * b_sh + b, b)
        a = jnp.where(ok, a * a_sh, a)
    aloc[...] = a.reshape(tm, d_rnn)
    hloc[...] = b.reshape(tm, d_rnn)

    def body(gi, carry):
        r0 = pl.multiple_of(gi * SUBLANES, SUBLANES)
        h = hloc[pl.ds(r0, SUBLANES), :] + aloc[pl.ds(r0, SUBLANES), :] * carry
        hloc[pl.ds(r0, SUBLANES), :] = h
        return h[SUBLANES - 1:SUBLANES, :]

    hcar[...] = lax.fori_loop(0, tm // SUBLANES, body, hcar[...])

    hg = (hloc[...] * gate).astype(BF16)
    x1_ref[...] = x + _dot(hg, wout_ref[...]) + bout_ref[...]

    ubuf[SUBLANES - hist:SUBLANES, :] = ubuf[SUBLANES + tm - hist:SUBLANES + tm, :]

    @pl.when(t == nt - 1)
    def _():
        convn_ref[0] = ubuf[SUBLANES - hist:SUBLANES, :]
        hn_ref[0] = hcar[...]


def _rglru_seq(x, conv0, h0, g, w_in, b_in, cw, cb, wga, bga, wgi, bgi, lam, w_out, b_out, *, tm):
    B, T, D = x.shape
    d_rnn = w_in.shape[1] // 2
    hist = CONV_W - 1
    assert T % tm == 0 and tm % SUBLANES == 0 and tm >= hist
    nt = T // tm
    x2 = x.reshape(B * T, D)
    full = lambda a: pl.BlockSpec(a.shape, lambda b, t: (0,) * a.ndim)
    row = lambda v: v.reshape(1, -1)
    args = (x2, conv0, h0.reshape(B, 1, d_rnn), row(g), w_in, row(b_in), cw, row(cb),
            wga, row(bga), wgi, row(bgi), row(lam), w_out, row(b_out))
    in_specs = [pl.BlockSpec((tm, D), lambda b, t: (b * nt + t, 0)),
                pl.BlockSpec((1, hist, d_rnn), lambda b, t: (b, 0, 0)),
                pl.BlockSpec((1, 1, d_rnn), lambda b, t: (b, 0, 0))]
    in_specs += [full(a) for a in args[3:]]
    x1, convn, hn = pl.pallas_call(
        functools.partial(_rglru_seq_kernel, tm=tm, d_rnn=d_rnn),
        grid=(B, nt),
        in_specs=in_specs,
        out_specs=[pl.BlockSpec((tm, D), lambda b, t: (b * nt + t, 0)),
                   pl.BlockSpec((1, hist, d_rnn), lambda b, t: (b, 0, 0)),
                   pl.BlockSpec((1, 1, d_rnn), lambda b, t: (b, 0, 0))],
        out_shape=[jax.ShapeDtypeStruct((B * T, D), F32),
                   jax.ShapeDtypeStruct((B, hist, d_rnn), F32),
                   jax.ShapeDtypeStruct((B, 1, d_rnn), F32)],
        scratch_shapes=[pltpu.VMEM((tm + SUBLANES, d_rnn), F32),
                        pltpu.VMEM((tm, d_rnn), F32),
                        pltpu.VMEM((tm, d_rnn), F32),
                        pltpu.VMEM((1, d_rnn), F32)],
        compiler_params=pltpu.CompilerParams(
            dimension_semantics=("arbitrary", "arbitrary"), vmem_limit_bytes=VMEM_LIMIT),
        name="rglru_seq",
    )(*args)
    return x1, convn, hn.reshape(B, d_rnn)


def _rglru_step_kernel(x_ref, conv_ref, h0_ref, g_ref, win_ref, bin_ref, cw_ref, cb_ref,
                       wga_ref, bga_ref, wgi_ref, bgi_ref, lam_ref, wout_ref, bout_ref,
                       x1_ref, convn_ref, hn_ref, *, d_rnn):
    hist = CONV_W - 1
    x = x_ref[...]
    xn = _rmsnorm(x, g_ref[...]).astype(BF16)
    proj = _dot(xn, win_ref[...]) + bin_ref[...]
    gate = _gelu_tanh(proj[:, :d_rnn])
    u = proj[:, d_rnn:]
    c = cb_ref[...] + conv_ref[0] * cw_ref[0:1, :]
    for k in range(1, hist):
        c = c + conv_ref[k] * cw_ref[k:k + 1, :]
    c = c + u * cw_ref[hist:hist + 1, :]
    a, b = _lru_coeffs(c, wga_ref, bga_ref[...], wgi_ref, bgi_ref[...], lam_ref[...])
    h = a * h0_ref[...] + b
    hn_ref[...] = h
    for k in range(hist - 1):
        convn_ref[k] = conv_ref[k + 1]
    convn_ref[hist - 1] = u
    x1_ref[...] = x + _dot((h * gate).astype(BF16), wout_ref[...]) + bout_ref[...]


def _rglru_step(x, conv0, h0, g, w_in, b_in, cw, cb, wga, bga, wgi, bgi, lam, w_out, b_out):
    N, D = x.shape
    d_rnn = w_in.shape[1] // 2
    hist = CONV_W - 1
    row = lambda v: v.reshape(1, -1)
    args = (x, conv0, h0, row(g), w_in, row(b_in), cw, row(cb),
            wga, row(bga), wgi, row(bgi), row(lam), w_out, row(b_out))
    return pl.pallas_call(
        functools.partial(_rglru_step_kernel, d_rnn=d_rnn),
        out_shape=[jax.ShapeDtypeStruct((N, D), F32),
                   jax.ShapeDtypeStruct((hist, N, d_rnn), F32),
                   jax.ShapeDtypeStruct((N, d_rnn), F32)],
        compiler_params=pltpu.CompilerParams(vmem_limit_bytes=VMEM_LIMIT),
        name="rglru_step",
    )(*args)


def _sqrelu_mlp(xn_bf, w1_ref, w2_ref, fc, lo=0, hi=None):
    hi = w1_ref.shape[1] if hi is None else hi
    acc = None
    for c0 in range(lo, hi, fc):
        h = jnp.maximum(_dot(xn_bf, w1_ref[:, c0:c0 + fc]), 0.0)
        part = _dot((h * h).astype(BF16), w2_ref[c0:c0 + fc, :])
        acc = part if acc is None else acc + part
    return acc


def _mlp_kv_kernel(x_ref, gm_ref, w1_ref, w2_ref, gkv_ref, wkvt_ref, gq_ref, wq_ref,
                   x2_ref, kt_ref, vt_ref, q_ref, *, fc):
    x = x_ref[...]
    x2 = x + _sqrelu_mlp(_rmsnorm(x, gm_ref[...]).astype(BF16), w1_ref, w2_ref, fc)
    x2_ref[...] = x2
    hd = kt_ref.shape[1]
    kvt = _dot_nt(wkvt_ref[...], _rmsnorm(x2, gkv_ref[...]).astype(BF16))
    kt_ref[0] = kvt[:hd, :]
    vt_ref[0] = kvt[hd:, :]
    q_ref[...] = _dot(_rmsnorm(x2, gq_ref[...]).astype(BF16), wq_ref[...])


def _mlp_kv(x, B, gm, w1, w2, gkv, wkvt, gq, wq, *, tm, fc=512):
    N, D = x.shape
    hd = wq.shape[1]
    T = N // B
    assert N == B * T and T % tm == 0
    nt = T // tm
    row = lambda v: v.reshape(1, -1)
    args = (x, row(gm), w1, w2, row(gkv), wkvt, row(gq), wq)
    full = lambda a: pl.BlockSpec(a.shape, lambda i: (0,) * a.ndim)
    tile = lambda w: pl.BlockSpec((tm, w), lambda i: (i, 0))
    tile_t = pl.BlockSpec((1, hd, tm), lambda i: (i // nt, 0, i % nt))
    return pl.pallas_call(
        functools.partial(_mlp_kv_kernel, fc=fc),
        grid=(N // tm,),
        in_specs=[tile(D)] + [full(a) for a in args[1:]],
        out_specs=[tile(D), tile_t, tile_t, tile(hd)],
        out_shape=[jax.ShapeDtypeStruct((N, D), F32),
                   jax.ShapeDtypeStruct((B, hd, T), F32),
                   jax.ShapeDtypeStruct((B, hd, T), F32),
                   jax.ShapeDtypeStruct((N, hd), F32)],
        compiler_params=pltpu.CompilerParams(
            dimension_semantics=("arbitrary",), vmem_limit_bytes=VMEM_LIMIT),
        name="mlp_kv",
    )(*args)


def _attn_out_mlp_kernel(x_ref, a_ref, wo_ref, gm_ref, w1_ref, w2_ref, go_ref, y_ref, *, fc):
    x3 = x_ref[...] + _dot(a_ref[...].astype(BF16), wo_ref[...])
    x4 = x3 + _sqrelu_mlp(_rmsnorm(x3, gm_ref[...]).astype(BF16), w1_ref, w2_ref, fc)
    y_ref[...] = _rmsnorm(x4, go_ref[...])


def _attn_out_mlp(x, attn, wo, gm, w1, w2, go, *, tm, fc=512):
    N, D = x.shape
    assert N % tm == 0
    row = lambda v: v.reshape(1, -1)
    args = (x, attn, wo, row(gm), w1, w2, row(go))
    full = lambda a: pl.BlockSpec(a.shape, lambda i: (0,) * a.ndim)
    tile = lambda w: pl.BlockSpec((tm, w), lambda i: (i, 0))
    return pl.pallas_call(
        functools.partial(_attn_out_mlp_kernel, fc=fc),
        grid=(N // tm,),
        in_specs=[tile(D), tile(attn.shape[1])] + [full(a) for a in args[2:]],
        out_specs=tile(D),
        out_shape=jax.ShapeDtypeStruct((N, D), F32),
        compiler_params=pltpu.CompilerParams(
            dimension_semantics=("arbitrary",), vmem_limit_bytes=VMEM_LIMIT),
        name="attn_out_mlp",
    )(*args)


def _topk_select(gate, allowed, axis=1):
    nb = gate.shape[axis]
    blk = lax.broadcasted_iota(jnp.int32, gate.shape, axis)
    g = jnp.where(allowed, gate, NEG)
    sel = jnp.zeros(gate.shape, jnp.bool_)
    for _ in range(MOBA_TOPK):
        m = jnp.max(g, axis=axis, keepdims=True)
        first = jnp.min(jnp.where(g == m, blk, nb), axis=axis, keepdims=True)
        pick = blk == first
        sel = jnp.logical_or(sel, pick)
        g = jnp.where(pick, -jnp.inf, g)
    return jnp.logical_and(sel, allowed)


def _block_sums(x, nb):
    blk = lax.broadcasted_iota(jnp.int32, (x.shape[0], nb), 1)
    out = jnp.zeros((x.shape[0], nb), F32)
    for j in range(nb):
        sj = jnp.sum(x[:, j * MOBA_BLOCK:(j + 1) * MOBA_BLOCK], axis=1, keepdims=True)
        out = jnp.where(blk == j, sj, out)
    return out


def _moba_gate_kernel(q_ref, kt_ref, bias_ref, kw, *, nb, nbp, tq):
    t = pl.program_id(2)
    blk_rows = MOBA_BLOCK

    @pl.when(t == 0)
    def _():
        km_t = _block_sums(kt_ref[0], nb) * (1.0 / blk_rows)
        km_t = jnp.concatenate([km_t, jnp.zeros((LANES, LANES - nb), F32)], axis=1)
        km = km_t.T[0:nbp, :]
        feat = lax.broadcasted_iota(jnp.int32, km.shape, 1)
        for hh in range(HEADS_PER_GROUP):
            own = jnp.logical_and(feat >= hh * HEAD_DIM, feat < (hh + 1) * HEAD_DIM)
            k_h = jnp.where(own, km, 0.0)
            k_hi = k_h.astype(BF16)
            k_lo = (k_h - k_hi.astype(F32)).astype(BF16)
            kw[hh, 0:nbp, :] = jnp.concatenate([k_hi, k_hi], axis=1)
            kw[hh, nbp:, :] = jnp.concatenate([k_lo, jnp.zeros_like(k_lo)], axis=1)

    q2 = q_ref[...]
    q_hi = q2.astype(BF16)
    q_lo = (q2 - q_hi.astype(F32)).astype(BF16)
    q_cat = jnp.concatenate([q_hi, q_lo], axis=1)
    blk_t = lax.broadcasted_iota(jnp.int32, (nbp, tq), 0)
    q_blk = (t * tq + lax.broadcasted_iota(jnp.int32, (nbp, tq), 1)) // blk_rows
    for hh in range(HEADS_PER_GROUP):
        parts = _dot_nt(kw[hh], q_cat)
        gate_t = parts[0:nbp, :] + parts[nbp:, :]
        sel_t = _topk_select(gate_t, blk_t < q_blk, axis=0)
        bias_t = jnp.where(jnp.logical_or(sel_t, blk_t == q_blk), 0.0, NEG)
        bias_t = jnp.concatenate([bias_t, jnp.zeros((LANES - nbp, tq), F32)], axis=0)
        bias_ref[hh] = bias_t.T.astype(BF16)


def _moba_gate(q, kt, *, tq=1024):
    N, HD = q.shape
    B, _, T = kt.shape
    assert N == B * T and T % tq == 0 and tq % MOBA_BLOCK == 0 and HD % LANES == 0
    nb = T // MOBA_BLOCK
    nbp = -(-nb // (2 * SUBLANES)) * (2 * SUBLANES)
    assert nbp <= LANES
    nt = T // tq
    return pl.pallas_call(
        functools.partial(_moba_gate_kernel, nb=nb, nbp=nbp, tq=tq),
        grid=(B, HD // LANES, nt),
        in_specs=[pl.BlockSpec((tq, LANES), lambda b, g, t: (b * nt + t, g)),
                  pl.BlockSpec((1, LANES, T), lambda b, g, t: (b, g, 0))],
        out_specs=pl.BlockSpec((HEADS_PER_GROUP, tq, LANES), lambda b, g, t: (g, b * nt + t, 0)),
        out_shape=jax.ShapeDtypeStruct((HD // HEAD_DIM, N, LANES), BF16),
        scratch_shapes=[pltpu.VMEM((HEADS_PER_GROUP, 2 * nbp, 2 * LANES), BF16)],
        compiler_params=pltpu.CompilerParams(
            dimension_semantics=("arbitrary", "arbitrary", "arbitrary"),
            vmem_limit_bytes=VMEM_LIMIT),
        name="moba_gate",
    )(q, kt)


def _moba_seq_kernel(q_ref, bias_ref, kt_ref, vt_ref, o_ref, ktaug, vtaug, sbuf_even, sbuf_odd, msbuf, *,
                     nb, bpc, scale):
    i = pl.program_id(2)
    blk_rows = MOBA_BLOCK
    kc = bpc * blk_rows
    nchunk = ktaug.shape[0]
    c_own = i // bpc
    heads = range(HEADS_PER_GROUP)

    @pl.when(i == 0)
    def _():
        key_blk = lax.broadcasted_iota(jnp.int32, (kc, LANES), 0) // blk_rows
        blk_lane = lax.broadcasted_iota(jnp.int32, (kc, LANES), 1)
        feat_row = lax.broadcasted_iota(jnp.int32, (LANES, kc), 0)
        for c in range(nchunk):
            ktaug[c, :, 0:LANES] = kt_ref[0, :, c * kc:(c + 1) * kc].T.astype(BF16)
            ktaug[c, :, LANES:] = jnp.where(blk_lane == key_blk + c * bpc, 1.0, 0.0).astype(BF16)
            vt_c = vt_ref[0, :, c * kc:(c + 1) * kc]
            for hh in heads:
                own = jnp.logical_and(feat_row >= hh * HEAD_DIM, feat_row < (hh + 1) * HEAD_DIM)
                ones_row = ((hh + 1) % HEADS_PER_GROUP) * HEAD_DIM
                vtaug[hh, c] = jnp.where(own, vt_c, jnp.where(feat_row == ones_row, 1.0, 0.0)).astype(BF16)

    def group_max(s, m):
        sm = jnp.max(s.reshape(kc // SUBLANES, SUBLANES, blk_rows), axis=0)
        return sm if m is None else jnp.maximum(m, sm)

    def pairwise(n, body, carry):
        carry = lax.fori_loop(0, n // 2, lambda t, cr: body(2 * t + 1, body(2 * t, cr)), carry)
        return lax.fori_loop(0, n % 2, lambda _, cr: body(n - 1, cr), carry)

    def make_q_augs():
        q2 = q_ref[...]
        lane = lax.broadcasted_iota(jnp.int32, q2.shape, 1)
        out = []
        for hh in heads:
            in_head = jnp.logical_and(lane >= hh * HEAD_DIM, lane < (hh + 1) * HEAD_DIM)
            qs = (jnp.where(in_head, q2, 0.0) * (scale * LOG2E)).astype(BF16)
            out.append(jnp.concatenate([qs, bias_ref[hh]], axis=1))
        return out

    def own_chunk_scores(s_cur, q_augs):
        k_id = lax.broadcasted_iota(jnp.int32, (kc, blk_rows), 0)
        q_id = lax.broadcasted_iota(jnp.int32, (kc, blk_rows), 1)
        own_off = (i - c_own * bpc) * blk_rows
        keep = jnp.logical_or(k_id - own_off <= q_id,
                              jnp.logical_or(k_id < own_off, k_id >= own_off + blk_rows))
        mruns = []
        for hh in heads:
            s = jnp.where(keep, _dot_nt(ktaug[c_own], q_augs[hh]), NEG)
            s_cur[hh, c_own] = s
            mruns.append(group_max(s, None))
        return tuple(mruns)

    def score_chunk(s_cur, c, q_augs, mruns):
        out = []
        for hh in heads:
            s = _dot_nt(ktaug[c], q_augs[hh])
            s_cur[hh, c] = s
            out.append(group_max(s, mruns[hh]))
        return tuple(out)

    def pv_chunk(s_prv, c, ms, accs):
        return tuple(accs[hh] + _dot(vtaug[hh, c], jnp.exp2(s_prv[hh, c] - ms[hh]).astype(BF16))
                     for hh in heads)

    def store_max(cur, mruns):
        for hh in heads:
            msbuf[cur, hh] = jnp.max(mruns[hh], axis=0, keepdims=True)

    def store_out(accs):
        feat = lax.broadcasted_iota(jnp.int32, (LANES, blk_rows), 0)
        o_t = None
        for hh in heads:
            ones_row = ((hh + 1) % HEADS_PER_GROUP) * HEAD_DIM
            oh = accs[hh] / accs[hh][ones_row:ones_row + 1, :]
            o_t = oh if o_t is None else jnp.where(feat >= hh * HEAD_DIM, oh, o_t)
        o_ref[...] = o_t.T

    zero_acc = (jnp.zeros((LANES, blk_rows), F32),) * HEADS_PER_GROUP

    def step(cur, s_cur, s_prv):
        prv = 1 - cur
        parity = i % 2 == cur

        @pl.when(jnp.logical_and(parity, i == 0))
        def _():
            store_max(cur, own_chunk_scores(s_cur, make_q_augs()))
            o_ref[...] = jnp.zeros(o_ref.shape, F32)

        @pl.when(jnp.logical_and(parity, jnp.logical_and(i > 0, i < nb)))
        def _():
            q_augs = make_q_augs()
            ms = [msbuf[prv, hh] for hh in heads]
            c_last = (i - 1) // bpc
            mruns = own_chunk_scores(s_cur, q_augs)
            accs = pv_chunk(s_prv, c_last, ms, zero_acc)

            def both(c, carry):
                mruns, accs = carry
                return score_chunk(s_cur, c, q_augs, mruns), pv_chunk(s_prv, c, ms, accs)

            mruns, accs = pairwise(c_last, both, (mruns, accs))
            n_extra = jnp.where(i % bpc == 0, 1, 0)
            mruns = lax.fori_loop(0, n_extra, lambda _, m: score_chunk(s_cur, c_last, q_augs, m), mruns)
            store_max(cur, mruns)
            store_out(accs)

        @pl.when(jnp.logical_and(parity, i == nb))
        def _():
            ms = [msbuf[prv, hh] for hh in heads]
            store_out(pairwise(nchunk, lambda c, a: pv_chunk(s_prv, c, ms, a), zero_acc))

    step(0, sbuf_even, sbuf_odd)
    step(1, sbuf_odd, sbuf_even)


def _moba_seq(q, bias, kt, vt, *, bpc=4):
    N, HD = q.shape
    B, _, T = kt.shape
    assert N == B * T and T % (bpc * MOBA_BLOCK) == 0 and HD % LANES == 0
    assert HEADS_PER_GROUP >= 2
    nb = T // MOBA_BLOCK
    nchunk = nb // bpc
    kc = bpc * MOBA_BLOCK
    q_blk = lambda b, g, i: (b * nb + jnp.minimum(i, nb - 1), g)
    o_blk = lambda b, g, i: (b * nb + jnp.maximum(i - 1, 0), g)
    resident = dict(pipeline_mode=pl.Buffered(1))
    return pl.pallas_call(
        functools.partial(_moba_seq_kernel, nb=nb, bpc=bpc, scale=HEAD_DIM ** -0.5),
        grid=(B, HD // LANES, nb + 1),
        in_specs=[pl.BlockSpec((MOBA_BLOCK, LANES), q_blk),
                  pl.BlockSpec((HEADS_PER_GROUP, MOBA_BLOCK, LANES),
                               lambda b, g, i: (g, b * nb + jnp.minimum(i, nb - 1), 0)),
                  pl.BlockSpec((1, LANES, T), lambda b, g, i: (b, g, 0), **resident),
                  pl.BlockSpec((1, LANES, T), lambda b, g, i: (b, g, 0), **resident)],
        out_specs=pl.BlockSpec((MOBA_BLOCK, LANES), o_blk),
        out_shape=jax.ShapeDtypeStruct((N, HD), F32),
        scratch_shapes=[pltpu.VMEM((nchunk, kc, 2 * LANES), BF16),
                        pltpu.VMEM((HEADS_PER_GROUP, nchunk, LANES, kc), BF16),
                        pltpu.VMEM((HEADS_PER_GROUP, nchunk, kc, MOBA_BLOCK), F32),
                        pltpu.VMEM((HEADS_PER_GROUP, nchunk, kc, MOBA_BLOCK), F32),
                        pltpu.VMEM((2, HEADS_PER_GROUP, 1, MOBA_BLOCK), F32)],
        compiler_params=pltpu.CompilerParams(
            dimension_semantics=("arbitrary", "arbitrary", "arbitrary"),
            vmem_limit_bytes=VMEM_LIMIT),
        name="moba_seq",
    )(q, bias, kt, vt)


def _tail_paged_kernel(pt_ref, x_ref, a_ref, wo_ref, gm_ref, w1_ref, w2_ref, go_ref,
                       qt_ref, knt_ref, vnt_ref, ck_hbm, cv_hbm, y_ref, ot_ref,
                       kbuf, vbuf, s_buf, ksem, vsem, *, fc, n_pages, page, nb, scale):
    b = pl.program_id(0)
    last = pl.num_programs(0) - 1
    HD = qt_ref.shape[0]
    d_ff = w1_ref.shape[1]

    def page_copy(src_hbm, dst, sem, seq, j):
        return pltpu.make_async_copy(src_hbm.at[pt_ref[seq, j]], dst.at[j], sem.at[0])

    def start_pages(src_hbm, dst, sem, seq):
        for j in range(n_pages):
            page_copy(src_hbm, dst, sem, seq, j).start()

    def wait_pages(src_hbm, dst, sem):
        for j in range(n_pages):
            page_copy(src_hbm, dst, sem, b, j).wait()

    def head_sums(x):
        return jnp.sum(x.reshape(N_HEADS, HEAD_DIM, x.shape[1]), axis=1)

    def head_bcast(x):
        return jnp.broadcast_to(x[:, None, :], (N_HEADS, HEAD_DIM, x.shape[1])).reshape(HD, x.shape[1])

    @pl.when(b == 0)
    def _():
        start_pages(ck_hbm, kbuf, ksem, b)
        start_pages(cv_hbm, vbuf, vsem, b)
        ot_ref[...] = jnp.zeros(ot_ref.shape, F32)

    seq_lane = lax.broadcasted_iota(jnp.int32, qt_ref.shape, 1) == b
    column = lambda ref: jnp.sum(jnp.where(seq_lane, ref[...], 0.0), axis=1, keepdims=True)
    q_col = column(qt_ref) * scale
    kn_col = column(knt_ref)
    vn_col = column(vnt_ref)
    q_b = jnp.broadcast_to(q_col, (HD, page))

    nxt = jnp.minimum(b + 1, last)
    wait_pages(ck_hbm, kbuf, ksem)
    for h in range(N_HEADS):
        rows = slice(h * HEAD_DIM, (h + 1) * HEAD_DIM)
        q_h = q_b[rows, :]
        for j in range(n_pages):
            s_buf[j, h:h + 1, :] = jnp.sum(kbuf[j, rows, :] * q_h, axis=0, keepdims=True)
    x3 = x_ref[...] + _dot(a_ref[...].astype(BF16), wo_ref[...])
    xn = _rmsnorm(x3, gm_ref[...]).astype(BF16)
    half = (d_ff // fc // 2) * fc
    mlp = _sqrelu_mlp(xn, w1_ref, w2_ref, fc, 0, half)
    start_pages(ck_hbm, kbuf, ksem, nxt)

    ppb = MOBA_BLOCK // page
    blk_id = lax.broadcasted_iota(jnp.int32, (N_HEADS, nb), 1)
    gate = jnp.zeros((N_HEADS, nb), F32)
    for k in range(nb):
        blk = s_buf[k * ppb]
        for j in range(k * ppb + 1, (k + 1) * ppb):
            blk = blk + s_buf[j]
        gate = jnp.where(blk_id == k, jnp.sum(blk, axis=1, keepdims=True) * (1.0 / MOBA_BLOCK), gate)
    sel = _topk_select(gate, jnp.ones(gate.shape, jnp.bool_))
    s_own = head_sums(q_col * kn_col)
    m = s_own
    for j in range(n_pages):
        sj = jnp.where(sel[:, j // ppb:j // ppb + 1], s_buf[j], NEG)
        s_buf[j] = sj
        m = jnp.maximum(m, jnp.max(sj, axis=1, keepdims=True))
    p_own = jnp.exp(s_own - m)
    l = p_own
    for j in range(n_pages):
        pj = jnp.exp(s_buf[j] - m)
        s_buf[j] = pj
        l = l + jnp.sum(pj, axis=1, keepdims=True)

    wait_pages(cv_hbm, vbuf, vsem)
    o_heads = []
    for h in range(N_HEADS):
        rows = slice(h * HEAD_DIM, (h + 1) * HEAD_DIM)
        acc = vbuf[0, rows, :] * s_buf[0, h:h + 1, :]
        for j in range(1, n_pages):
            acc = acc + vbuf[j, rows, :] * s_buf[j, h:h + 1, :]
        o_heads.append(jnp.sum(acc, axis=1, keepdims=True))
    x4 = x3 + mlp + _sqrelu_mlp(xn, w1_ref, w2_ref, fc, half, d_ff)
    y_ref[...] = _rmsnorm(x4, go_ref[...])
    start_pages(cv_hbm, vbuf, vsem, nxt)
    o_col = jnp.concatenate(o_heads, axis=0) + head_bcast(p_own) * vn_col
    o_col = o_col / head_bcast(l)
    ot_ref[...] = jnp.where(seq_lane, o_col, ot_ref[...])

    @pl.when(b == last)
    def _():
        wait_pages(ck_hbm, kbuf, ksem)
        wait_pages(cv_hbm, vbuf, vsem)


def _tail_paged(x, attn, wo, gm, w1, w2, go, qt, knt, vnt, ck, cv, page_table, *, fc=512):
    N, D = x.shape
    HD, NS = qt.shape
    page = ck.shape[2]
    n_pages = page_table.shape[1]
    past_len = n_pages * page
    assert past_len % MOBA_BLOCK == 0 and MOBA_BLOCK % page == 0 and page % LANES == 0
    nb = past_len // MOBA_BLOCK
    assert nb >= 1 and N % NS == 0 and (N // NS) % SUBLANES == 0
    tm = N // NS
    row = lambda v: v.reshape(1, -1)
    weights = (wo, row(gm), w1, w2, row(go))
    full = lambda a: pl.BlockSpec(a.shape, lambda b, pt: (0,) * a.ndim)
    tile = lambda w: pl.BlockSpec((tm, w), lambda b, pt: (b, 0))
    cols = pl.BlockSpec((HD, NS), lambda b, pt: (0, 0))
    return pl.pallas_call(
        functools.partial(_tail_paged_kernel, fc=fc, n_pages=n_pages, page=page, nb=nb,
                          scale=HEAD_DIM ** -0.5),
        grid_spec=pltpu.PrefetchScalarGridSpec(
            num_scalar_prefetch=1,
            grid=(NS,),
            in_specs=[tile(D), tile(attn.shape[1])] + [full(a) for a in weights] + [cols, cols, cols,
                      pl.BlockSpec(memory_space=pl.ANY),
                      pl.BlockSpec(memory_space=pl.ANY)],
            out_specs=[tile(D), cols],
            scratch_shapes=[pltpu.VMEM((n_pages, HD, page), F32),
                            pltpu.VMEM((n_pages, HD, page), F32),
                            pltpu.VMEM((n_pages, N_HEADS, page), F32),
                            pltpu.SemaphoreType.DMA((1,)),
                            pltpu.SemaphoreType.DMA((1,))]),
        out_shape=[jax.ShapeDtypeStruct((N, D), F32), jax.ShapeDtypeStruct((HD, NS), F32)],
        compiler_params=pltpu.CompilerParams(
            dimension_semantics=("arbitrary",), vmem_limit_bytes=VMEM_LIMIT),
        name="tail_paged",
    )(page_table, x, attn, *weights, qt, knt, vnt, ck, cv)


def kernel(x_prompt, x_sample, state_conv, state_h, cache_k, cache_v, page_table, norm_mix, norm_mlp, w_ff1, w_ff2, w_rg_in, b_rg_in, conv_w, conv_b, w_gate_a, b_gate_a, w_gate_i, b_gate_i, lru_lambda, w_rg_out, b_rg_out, norm_kv, w_kv, w_q, w_o, norm_out):
    B, T, D = x_prompt.shape
    NS, TS, _ = x_sample.shape
    depth = norm_mix.shape[0]
    assert depth == 2 and w_rg_in.shape[0] == 1 and w_q.shape[0] == 1 and TS == 1
    HD = N_HEADS * HEAD_DIM
    d_rnn = w_rg_in.shape[2] // 2
    hist = CONV_W - 1
    n_phys, page = cache_k.shape[0], cache_k.shape[1]
    bf = lambda w: w.astype(BF16)

    rg = (norm_mix[0], bf(w_rg_in[0]), b_rg_in[0], conv_w[0], conv_b[0],
          bf(w_gate_a[0]), b_gate_a[0].reshape(-1), bf(w_gate_i[0]), b_gate_i[0].reshape(-1),
          lru_lambda[0], bf(w_rg_out[0]), b_rg_out[0])
    mlp0 = (norm_mlp[0], bf(w_ff1[0]), bf(w_ff2[0]), norm_kv, bf(w_kv.T), norm_mix[1], bf(w_q[0]))
    mlp1 = (bf(w_o[0]), norm_mlp[1], bf(w_ff1[1]), bf(w_ff2[1]), norm_out)

    def heads_last(xt):
        lead = xt.shape[:-2]
        xt = xt.reshape(lead + (N_HEADS, HEAD_DIM, xt.shape[-1]))
        return jnp.moveaxis(xt, -1, -3)

    x1, conv_p, h_p = _rglru_seq(x_prompt, jnp.zeros((B, hist, d_rnn), F32),
                                 jnp.zeros((B, d_rnn), F32), *rg, tm=256)
    x2, kt_p, vt_p, q_p = _mlp_kv(x1, B, *mlp0, tm=256)
    attn_p = _moba_seq(q_p, _moba_gate(q_p, kt_p, tq=min(T, 2048)), kt_p, vt_p)

    xs = x_sample.reshape(NS, D)
    x1s, conv_s, h_s = _rglru_step(xs, jnp.swapaxes(state_conv[0], 0, 1), state_h[0], *rg)
    x2s, kt_s, vt_s, q_s = _mlp_kv(x1s, 1, *mlp0, tm=NS)
    ck = jnp.transpose(cache_k, (0, 2, 3, 1)).reshape(n_phys, HD, page)
    cv = jnp.transpose(cache_v, (0, 2, 3, 1)).reshape(n_phys, HD, page)
    y_p, attn_st = _tail_paged(x2, attn_p, *mlp1, q_s.T, kt_s[0], vt_s[0], ck, cv, page_table)
    y_s = _attn_out_mlp(x2s, attn_st.T, *mlp1, tm=NS)

    return (y_p.reshape(B, T, D), y_s.reshape(NS, 1, D),
            conv_p[None], h_p[None], heads_last(kt_p), heads_last(vt_p),
            jnp.swapaxes(conv_s, 0, 1)[None], h_s[None],
            heads_last(kt_s[0])[:, None], heads_last(vt_s[0])[:, None])
```

```python
import functools
import math

import jax
import jax.numpy as jnp
from jax import lax
from jax.experimental import pallas as pl
from jax.experimental.pallas import tpu as pltpu

N_HEADS = 16
HEAD_DIM = 64
N_LRU_BLOCKS = 4
CONV_W = 4
LRU_C = 8.0
MOBA_BLOCK = 256
MOBA_TOPK = 3
EPS = 1e-6
NEG = -1e30
LOG2E = math.log2(math.e)

LANES = 128
SUBLANES = 8
HEADS_PER_GROUP = LANES // HEAD_DIM
VMEM_LIMIT = 56 * 1024 * 1024

BF16 = jnp.bfloat16
F32 = jnp.float32


def _rmsnorm(x, g):
    return x * lax.rsqrt(jnp.mean(x * x, axis=-1, keepdims=True) + EPS) * g


def _rmsnorm_mxu(x, g):
    d = x.shape[1]
    ss = jnp.dot((x * x).astype(BF16), jnp.ones((d, LANES), BF16), preferred_element_type=F32)
    inv = lax.rsqrt(ss * (1.0 / d) + EPS)
    return x * jnp.concatenate([inv] * (d // LANES), axis=1) * g


def _dot(a, b, precision=None):
    return jnp.dot(a, b, preferred_element_type=F32, precision=precision)


def _dot_nt(a, b):
    return lax.dot_general(a, b, (((1,), (1,)), ((), ())), preferred_element_type=F32)


def _gelu_tanh(x):
    c = math.sqrt(2.0 / math.pi)
    return x * (0.5 * (1.0 + jnp.tanh(c * (x + 0.044715 * (x * x * x)))))


def _softplus(y):
    return jnp.maximum(y, 0.0) + jnp.log1p(jnp.exp(-jnp.abs(y)))


def _neg_expm1(x):
    return jnp.tanh(-0.5 * x) * (jnp.exp(x) + 1.0)


def _lru_coeffs(c, wga_ref, bga, wgi_ref, bgi, lam):
    blk = c.shape[1] // N_LRU_BLOCKS
    cb = c.astype(BF16)
    ra, ri = [], []
    for n in range(N_LRU_BLOCKS):
        cn = cb[:, n * blk:(n + 1) * blk]
        ra.append(_dot(cn, wga_ref[n]))
        ri.append(_dot(cn, wgi_ref[n]))
    r = jax.nn.sigmoid(jnp.concatenate(ra, axis=1) + bga)
    i = jax.nn.sigmoid(jnp.concatenate(ri, axis=1) + bgi)
    log_a = -LRU_C * r * _softplus(-lam)
    a = jnp.exp(log_a)
    mult = jnp.sqrt(_neg_expm1(2.0 * log_a))
    return a, mult * (i * c)


def _rglru_seq_kernel(x_ref, conv0_ref, h0_ref, g_ref, win_ref, bin_ref, cw_ref, cb_ref,
                      wga_ref, bga_ref, wgi_ref, bgi_ref, lam_ref, wout_ref, bout_ref,
                      x1_ref, convn_ref, hn_ref,
                      ubuf, aloc, hloc, hcar, *, tm, d_rnn):
    t = pl.program_id(1)
    nt = pl.num_programs(1)
    hist = CONV_W - 1

    @pl.when(t == 0)
    def _():
        ubuf[SUBLANES - hist:SUBLANES, :] = conv0_ref[0]
        hcar[...] = h0_ref[0]

    x = x_ref[...]
    xn = _rmsnorm_mxu(x, g_ref[...]).astype(BF16)
    proj = _dot(xn, win_ref[...]) + bin_ref[...]
    gate = _gelu_tanh(proj[:, :d_rnn])
    u = proj[:, d_rnn:]
    ubuf[SUBLANES:SUBLANES + tm, :] = u

    c = cb_ref[...] + ubuf[SUBLANES - hist:SUBLANES - hist + tm, :] * cw_ref[0:1, :]
    for k in range(1, CONV_W):
        off = SUBLANES - hist + k
        c = c + ubuf[off:off + tm, :] * cw_ref[k:k + 1, :]

    a, b = _lru_coeffs(c, wga_ref, bga_ref[...], wgi_ref, bgi_ref[...], lam_ref[...])

    a = a.reshape(tm // SUBLANES, SUBLANES, d_rnn)
    b = b.reshape(tm // SUBLANES, SUBLANES, d_rnn)
    row = lax.broadcasted_iota(jnp.int32, a.shape, 1)
    for s in (1, 2, 4):
        ok = row >= s
        a_sh = pltpu.roll(a, s, 1)
        b_sh = pltpu.roll(b, s, 1)
        b = jnp.where(ok, a * b_sh + b, b)
        a = jnp.where(ok, a * a_sh, a)
    aloc[...] = a.reshape(tm, d_rnn)
    hloc[...] = b.reshape(tm, d_rnn)

    def body(gi, carry):
        r0 = pl.multiple_of(gi * SUBLANES, SUBLANES)
        h = hloc[pl.ds(r0, SUBLANES), :] + aloc[pl.ds(r0, SUBLANES), :] * carry
        hloc[pl.ds(r0, SUBLANES), :] = h
        return h[SUBLANES - 1:SUBLANES, :]

    hcar[...] = lax.fori_loop(0, tm // SUBLANES, body, hcar[...])

    hg = (hloc[...] * gate).astype(BF16)
    x1_ref[...] = x + _dot(hg, wout_ref[...]) + bout_ref[...]

    ubuf[SUBLANES - hist:SUBLANES, :] = ubuf[SUBLANES + tm - hist:SUBLANES + tm, :]

    @pl.when(t == nt - 1)
    def _():
        convn_ref[0] = ubuf[SUBLANES - hist:SUBLANES, :]
        hn_ref[0] = hcar[...]


def _rglru_seq(x, conv0, h0, g, w_in, b_in, cw, cb, wga, bga, wgi, bgi, lam, w_out, b_out, *, tm):
    B, T, D = x.shape
    d_rnn = w_in.shape[1] // 2
    hist = CONV_W - 1
    assert T % tm == 0 and tm % SUBLANES == 0 and tm >= hist
    nt = T // tm
    x2 = x.reshape(B * T, D)
    full = lambda a: pl.BlockSpec(a.shape, lambda b, t: (0,) * a.ndim)
    row = lambda v: v.reshape(1, -1)
    args = (x2, conv0, h0.reshape(B, 1, d_rnn), row(g), w_in, row(b_in), cw, row(cb),
            wga, row(bga), wgi, row(bgi), row(lam), w_out, row(b_out))
    in_specs = [pl.BlockSpec((tm, D), lambda b, t: (b * nt + t, 0)),
                pl.BlockSpec((1, hist, d_rnn), lambda b, t: (b, 0, 0)),
                pl.BlockSpec((1, 1, d_rnn), lambda b, t: (b, 0, 0))]
    in_specs += [full(a) for a in args[3:]]
    x1, convn, hn = pl.pallas_call(
        functools.partial(_rglru_seq_kernel, tm=tm, d_rnn=d_rnn),
        grid=(B, nt),
        in_specs=in_specs,
        out_specs=[pl.BlockSpec((tm, D), lambda b, t: (b * nt + t, 0)),
                   pl.BlockSpec((1, hist, d_rnn), lambda b, t: (b, 0, 0)),
                   pl.BlockSpec((1, 1, d_rnn), lambda b, t: (b, 0, 0))],
        out_shape=[jax.ShapeDtypeStruct((B * T, D), F32),
                   jax.ShapeDtypeStruct((B, hist, d_rnn), F32),
                   jax.ShapeDtypeStruct((B, 1, d_rnn), F32)],
        scratch_shapes=[pltpu.VMEM((tm + SUBLANES, d_rnn), F32),
                        pltpu.VMEM((tm, d_rnn), F32),
                        pltpu.VMEM((tm, d_rnn), F32),
                        pltpu.VMEM((1, d_rnn), F32)],
        compiler_params=pltpu.CompilerParams(
            dimension_semantics=("arbitrary", "arbitrary"), vmem_limit_bytes=VMEM_LIMIT),
        name="rglru_seq",
    )(*args)
    return x1, convn, hn.reshape(B, d_rnn)


def _rglru_step_kernel(x_ref, conv_ref, h0_ref, g_ref, win_ref, bin_ref, cw_ref, cb_ref,
                       wga_ref, bga_ref, wgi_ref, bgi_ref, lam_ref, wout_ref, bout_ref,
                       x1_ref, convn_ref, hn_ref, *, d_rnn):
    hist = CONV_W - 1
    x = x_ref[...]
    xn = _rmsnorm(x, g_ref[...]).astype(BF16)
    proj = _dot(xn, win_ref[...]) + bin_ref[...]
    gate = _gelu_tanh(proj[:, :d_rnn])
    u = proj[:, d_rnn:]
    c = cb_ref[...] + conv_ref[0] * cw_ref[0:1, :]
    for k in range(1, hist):
        c = c + conv_ref[k] * cw_ref[k:k + 1, :]
    c = c + u * cw_ref[hist:hist + 1, :]
    a, b = _lru_coeffs(c, wga_ref, bga_ref[...], wgi_ref, bgi_ref[...], lam_ref[...])
    h = a * h0_ref[...] + b
    hn_ref[...] = h
    for k in range(hist - 1):
        convn_ref[k] = conv_ref[k + 1]
    convn_ref[hist - 1] = u
    x1_ref[...] = x + _dot((h * gate).astype(BF16), wout_ref[...]) + bout_ref[...]


def _rglru_step(x, conv0, h0, g, w_in, b_in, cw, cb, wga, bga, wgi, bgi, lam, w_out, b_out):
    N, D = x.shape
    d_rnn = w_in.shape[1] // 2
    hist = CONV_W - 1
    row = lambda v: v.reshape(1, -1)
    args = (x, conv0, h0, row(g), w_in, row(b_in), cw, row(cb),
            wga, row(bga), wgi, row(bgi), row(lam), w_out, row(b_out))
    return pl.pallas_call(
        functools.partial(_rglru_step_kernel, d_rnn=d_rnn),
        out_shape=[jax.ShapeDtypeStruct((N, D), F32),
                   jax.ShapeDtypeStruct((hist, N, d_rnn), F32),
                   jax.ShapeDtypeStruct((N, d_rnn), F32)],
        compiler_params=pltpu.CompilerParams(vmem_limit_bytes=VMEM_LIMIT),
        name="rglru_step",
    )(*args)


def _sqrelu_mlp(xn_bf, w1_ref, w2_ref, fc, lo=0, hi=None):
    hi = w1_ref.shape[1] if hi is None else hi
    acc = None
    for c0 in range(lo, hi, fc):
        h = jnp.maximum(_dot(xn_bf, w1_ref[:, c0:c0 + fc]), 0.0)
        part = _dot((h * h).astype(BF16), w2_ref[c0:c0 + fc, :])
        acc = part if acc is None else acc + part
    return acc


def _mlp_kv_kernel(x_ref, gm_ref, w1_ref, w2_ref, gkv_ref, wkvt_ref, gq_ref, wq_ref,
                   x2_ref, kt_ref, vt_ref, q_ref, *, fc):
    x = x_ref[...]
    x2 = x + _sqrelu_mlp(_rmsnorm(x, gm_ref[...]).astype(BF16), w1_ref, w2_ref, fc)
    x2_ref[...] = x2
    hd = kt_ref.shape[1]
    kvt = _dot_nt(wkvt_ref[...], _rmsnorm(x2, gkv_ref[...]).astype(BF16))
    kt_ref[0] = kvt[:hd, :]
    vt_ref[0] = kvt[hd:, :]
    q_ref[...] = _dot(_rmsnorm(x2, gq_ref[...]).astype(BF16), wq_ref[...])


def _mlp_kv(x, B, gm, w1, w2, gkv, wkvt, gq, wq, *, tm, fc=512):
    N, D = x.shape
    hd = wq.shape[1]
    T = N // B
    assert N == B * T and T % tm == 0
    nt = T // tm
    row = lambda v: v.reshape(1, -1)
    args = (x, row(gm), w1, w2, row(gkv), wkvt, row(gq), wq)
    full = lambda a: pl.BlockSpec(a.shape, lambda i: (0,) * a.ndim)
    tile = lambda w: pl.BlockSpec((tm, w), lambda i: (i, 0))
    tile_t = pl.BlockSpec((1, hd, tm), lambda i: (i // nt, 0, i % nt))
    return pl.pallas_call(
        functools.partial(_mlp_kv_kernel, fc=fc),
        grid=(N // tm,),
        in_specs=[tile(D)] + [full(a) for a in args[1:]],
        out_specs=[tile(D), tile_t, tile_t, tile(hd)],
        out_shape=[jax.ShapeDtypeStruct((N, D), F32),
                   jax.ShapeDtypeStruct((B, hd, T), F32),
                   jax.ShapeDtypeStruct((B, hd, T), F32),
                   jax.ShapeDtypeStruct((N, hd), F32)],
        compiler_params=pltpu.CompilerParams(
            dimension_semantics=("arbitrary",), vmem_limit_bytes=VMEM_LIMIT),
        name="mlp_kv",
    )(*args)


def _attn_out_mlp_kernel(x_ref, a_ref, wo_ref, gm_ref, w1_ref, w2_ref, go_ref, y_ref, *, fc):
    x3 = x_ref[...] + _dot(a_ref[...].astype(BF16), wo_ref[...])
    x4 = x3 + _sqrelu_mlp(_rmsnorm(x3, gm_ref[...]).astype(BF16), w1_ref, w2_ref, fc)
    y_ref[...] = _rmsnorm(x4, go_ref[...])


def _attn_out_mlp(x, attn, wo, gm, w1, w2, go, *, tm, fc=512):
    N, D = x.shape
    assert N % tm == 0
    row = lambda v: v.reshape(1, -1)
    args = (x, attn, wo, row(gm), w1, w2, row(go))
    full = lambda a: pl.BlockSpec(a.shape, lambda i: (0,) * a.ndim)
    tile = lambda w: pl.BlockSpec((tm, w), lambda i: (i, 0))
    return pl.pallas_call(
        functools.partial(_attn_out_mlp_kernel, fc=fc),
        grid=(N // tm,),
        in_specs=[tile(D), tile(attn.shape[1])] + [full(a) for a in args[2:]],
        out_specs=tile(D),
        out_shape=jax.ShapeDtypeStruct((N, D), F32),
        compiler_params=pltpu.CompilerParams(
            dimension_semantics=("arbitrary",), vmem_limit_bytes=VMEM_LIMIT),
        name="attn_out_mlp",
    )(*args)


def _topk_select(gate, allowed, axis=1):
    nb = gate.shape[axis]
    blk = lax.broadcasted_iota(jnp.int32, gate.shape, axis)
    g = jnp.where(allowed, gate, NEG)
    sel = jnp.zeros(gate.shape, jnp.bool_)
    for _ in range(MOBA_TOPK):
        m = jnp.max(g, axis=axis, keepdims=True)
        first = jnp.min(jnp.where(g == m, blk, nb), axis=axis, keepdims=True)
        pick = blk == first
        sel = jnp.logical_or(sel, pick)
        g = jnp.where(pick, -jnp.inf, g)
    return jnp.logical_and(sel, allowed)


def _block_sums(x, nb):
    blk = lax.broadcasted_iota(jnp.int32, (x.shape[0], nb), 1)
    out = jnp.zeros((x.shape[0], nb), F32)
    for j in range(nb):
        sj = jnp.sum(x[:, j * MOBA_BLOCK:(j + 1) * MOBA_BLOCK], axis=1, keepdims=True)
        out = jnp.where(blk == j, sj, out)
    return out


def _moba_gate_kernel(q_ref, kt_ref, bias_ref, kw, *, nb, nbp, tq):
    t = pl.program_id(2)
    blk_rows = MOBA_BLOCK

    @pl.when(t == 0)
    def _():
        km_t = _block_sums(kt_ref[0], nb) * (1.0 / blk_rows)
        km_t = jnp.concatenate([km_t, jnp.zeros((LANES, LANES - nb), F32)], axis=1)
        km = km_t.T[0:nbp, :]
        feat = lax.broadcasted_iota(jnp.int32, km.shape, 1)
        for hh in range(HEADS_PER_GROUP):
            own = jnp.logical_and(feat >= hh * HEAD_DIM, feat < (hh + 1) * HEAD_DIM)
            k_h = jnp.where(own, km, 0.0)
            k_hi = k_h.astype(BF16)
            k_lo = (k_h - k_hi.astype(F32)).astype(BF16)
            kw[hh, 0:nbp, :] = jnp.concatenate([k_hi, k_hi], axis=1)
            kw[hh, nbp:, :] = jnp.concatenate([k_lo, jnp.zeros_like(k_lo)], axis=1)

    q2 = q_ref[...]
    q_hi = q2.astype(BF16)
    q_lo = (q2 - q_hi.astype(F32)).astype(BF16)
    q_cat = jnp.concatenate([q_hi, q_lo], axis=1)
    blk_t = lax.broadcasted_iota(jnp.int32, (nbp, tq), 0)
    q_blk = (t * tq + lax.broadcasted_iota(jnp.int32, (nbp, tq), 1)) // blk_rows
    for hh in range(HEADS_PER_GROUP):
        parts = _dot_nt(kw[hh], q_cat)
        gate_t = parts[0:nbp, :] + parts[nbp:, :]
        sel_t = _topk_select(gate_t, blk_t < q_blk, axis=0)
        bias_t = jnp.where(jnp.logical_or(sel_t, blk_t == q_blk), 0.0, NEG)
        bias_t = jnp.concatenate([bias_t, jnp.zeros((LANES - nbp, tq), F32)], axis=0)
        bias_ref[hh] = bias_t.T.astype(BF16)


def _moba_gate(q, kt, *, tq=1024):
    N, HD = q.shape
    B, _, T = kt.shape
    assert N == B * T and T % tq == 0 and tq % MOBA_BLOCK == 0 and HD % LANES == 0
    nb = T // MOBA_BLOCK
    nbp = -(-nb // (2 * SUBLANES)) * (2 * SUBLANES)
    assert nbp <= LANES
    nt = T // tq
    return pl.pallas_call(
        functools.partial(_moba_gate_kernel, nb=nb, nbp=nbp, tq=tq),
        grid=(B, HD // LANES, nt),
        in_specs=[pl.BlockSpec((tq, LANES), lambda b, g, t: (b * nt + t, g)),
                  pl.BlockSpec((1, LANES, T), lambda b, g, t: (b, g, 0))],
        out_specs=pl.BlockSpec((HEADS_PER_GROUP, tq, LANES), lambda b, g, t: (g, b * nt + t, 0)),
        out_shape=jax.ShapeDtypeStruct((HD // HEAD_DIM, N, LANES), BF16),
        scratch_shapes=[pltpu.VMEM((HEADS_PER_GROUP, 2 * nbp, 2 * LANES), BF16)],
        compiler_params=pltpu.CompilerParams(
            dimension_semantics=("arbitrary", "arbitrary", "arbitrary"),
            vmem_limit_bytes=VMEM_LIMIT),
        name="moba_gate",
    )(q, kt)


def _moba_seq_kernel(q_ref, bias_ref, kt_ref, vt_ref, o_ref, ktaug, vtaug, sbuf_even, sbuf_odd, msbuf, *,
                     nb, bpc, scale):
    i = pl.program_id(2)
    blk_rows = MOBA_BLOCK
    kc = bpc * blk_rows
    nchunk = ktaug.shape[0]
    c_own = i // bpc
    heads = range(HEADS_PER_GROUP)

    @pl.when(i == 0)
    def _():
        key_blk = lax.broadcasted_iota(jnp.int32, (kc, LANES), 0) // blk_rows
        blk_lane = lax.broadcasted_iota(jnp.int32, (kc, LANES), 1)
        feat_row = lax.broadcasted_iota(jnp.int32, (LANES, kc), 0)
        for c in range(nchunk):
            ktaug[c, :, 0:LANES] = kt_ref[0, :, c * kc:(c + 1) * kc].T.astype(BF16)
            ktaug[c, :, LANES:] = jnp.where(blk_lane == key_blk + c * bpc, 1.0, 0.0).astype(BF16)
            vt_c = vt_ref[0, :, c * kc:(c + 1) * kc]
            for hh in heads:
                own = jnp.logical_and(feat_row >= hh * HEAD_DIM, feat_row < (hh + 1) * HEAD_DIM)
                ones_row = ((hh + 1) % HEADS_PER_GROUP) * HEAD_DIM
                vtaug[hh, c] = jnp.where(own, vt_c, jnp.where(feat_row == ones_row, 1.0, 0.0)).astype(BF16)

    def group_max(s, m):
        sm = jnp.max(s.reshape(kc // SUBLANES, SUBLANES, blk_rows), axis=0)
        return sm if m is None else jnp.maximum(m, sm)

    def pairwise(n, body, carry):
        carry = lax.fori_loop(0, n // 2, lambda t, cr: body(2 * t + 1, body(2 * t, cr)), carry)
        return lax.fori_loop(0, n % 2, lambda _, cr: body(n - 1, cr), carry)

    def make_q_augs():
        q2 = q_ref[...]
        lane = lax.broadcasted_iota(jnp.int32, q2.shape, 1)
        out = []
        for hh in heads:
            in_head = jnp.logical_and(lane >= hh * HEAD_DIM, lane < (hh + 1) * HEAD_DIM)
            qs = (jnp.where(in_head, q2, 0.0) * (scale * LOG2E)).astype(BF16)
            out.append(jnp.concatenate([qs, bias_ref[hh]], axis=1))
        return out

    def own_chunk_scores(s_cur, q_augs):
        k_id = lax.broadcasted_iota(jnp.int32, (kc, blk_rows), 0)
        q_id = lax.broadcasted_iota(jnp.int32, (kc, blk_rows), 1)
        own_off = (i - c_own * bpc) * blk_rows
        keep = jnp.logical_or(k_id - own_off <= q_id,
                              jnp.logical_or(k_id < own_off, k_id >= own_off + blk_rows))
        mruns = []
        for hh in heads:
            s = jnp.where(keep, _dot_nt(ktaug[c_own], q_augs[hh]), NEG)
            s_cur[hh, c_own] = s
            mruns.append(group_max(s, None))
        return tuple(mruns)

    def score_chunk(s_cur, c, q_augs, mruns):
        out = []
        for hh in heads:
            s = _dot_nt(ktaug[c], q_augs[hh])
            s_cur[hh, c] = s
            out.append(group_max(s, mruns[hh]))
        return tuple(out)

    def pv_chunk(s_prv, c, ms, accs):
        return tuple(accs[hh] + _dot(vtaug[hh, c], jnp.exp2(s_prv[hh, c] - ms[hh]).astype(BF16))
                     for hh in heads)

    def store_max(cur, mruns):
        for hh in heads:
            msbuf[cur, hh] = jnp.max(mruns[hh], axis=0, keepdims=True)

    def store_out(accs):
        feat = lax.broadcasted_iota(jnp.int32, (LANES, blk_rows), 0)
        o_t = None
        for hh in heads:
            ones_row = ((hh + 1) % HEADS_PER_GROUP) * HEAD_DIM
            oh = accs[hh] / accs[hh][ones_row:ones_row + 1, :]
            o_t = oh if o_t is None else jnp.where(feat >= hh * HEAD_DIM, oh, o_t)
        o_ref[...] = o_t.T

    zero_acc = (jnp.zeros((LANES, blk_rows), F32),) * HEADS_PER_GROUP

    def step(cur, s_cur, s_prv):
        prv = 1 - cur
        parity = i % 2 == cur

        @pl.when(jnp.logical_and(parity, i == 0))
        def _():
            store_max(cur, own_chunk_scores(s_cur, make_q_augs()))
            o_ref[...] = jnp.zeros(o_ref.shape, F32)

        @pl.when(jnp.logical_and(parity, jnp.logical_and(i > 0, i < nb)))
        def _():
            q_augs = make_q_augs()
            ms = [msbuf[prv, hh] for hh in heads]
            c_last = (i - 1) // bpc
            mruns = own_chunk_scores(s_cur, q_augs)
            accs = pv_chunk(s_prv, c_last, ms, zero_acc)

            def both(c, carry):
                mruns, accs = carry
                return score_chunk(s_cur, c, q_augs, mruns), pv_chunk(s_prv, c, ms, accs)

            mruns, accs = pairwise(c_last, both, (mruns, accs))
            n_extra = jnp.where(i % bpc == 0, 1, 0)
            mruns = lax.fori_loop(0, n_extra, lambda _, m: score_chunk(s_cur, c_last, q_augs, m), mruns)
            store_max(cur, mruns)
            store_out(accs)

        @pl.when(jnp.logical_and(parity, i == nb))
        def _():
            ms = [msbuf[prv, hh] for hh in heads]
            store_out(pairwise(nchunk, lambda c, a: pv_chunk(s_prv, c, ms, a), zero_acc))

    step(0, sbuf_even, sbuf_odd)
    step(1, sbuf_odd, sbuf_even)


def _moba_seq(q, bias, kt, vt, *, bpc=4):
    N, HD = q.shape
    B, _, T = kt.shape
    assert N == B * T and T % (bpc * MOBA_BLOCK) == 0 and HD % LANES == 0
    assert HEADS_PER_GROUP >= 2
    nb = T // MOBA_BLOCK
    nchunk = nb // bpc
    kc = bpc * MOBA_BLOCK
    q_blk = lambda b, g, i: (b * nb + jnp.minimum(i, nb - 1), g)
    o_blk = lambda b, g, i: (b * nb + jnp.maximum(i - 1, 0), g)
    resident = dict(pipeline_mode=pl.Buffered(1))
    return pl.pallas_call(
        functools.partial(_moba_seq_kernel, nb=nb, bpc=bpc, scale=HEAD_DIM ** -0.5),
        grid=(B, HD // LANES, nb + 1),
        in_specs=[pl.BlockSpec((MOBA_BLOCK, LANES), q_blk),
                  pl.BlockSpec((HEADS_PER_GROUP, MOBA_BLOCK, LANES),
                               lambda b, g, i: (g, b * nb + jnp.minimum(i, nb - 1), 0)),
                  pl.BlockSpec((1, LANES, T), lambda b, g, i: (b, g, 0), **resident),
                  pl.BlockSpec((1, LANES, T), lambda b, g, i: (b, g, 0), **resident)],
        out_specs=pl.BlockSpec((MOBA_BLOCK, LANES), o_blk),
        out_shape=jax.ShapeDtypeStruct((N, HD), F32),
        scratch_shapes=[pltpu.VMEM((nchunk, kc, 2 * LANES), BF16),
                        pltpu.VMEM((HEADS_PER_GROUP, nchunk, LANES, kc), BF16),
                        pltpu.VMEM((HEADS_PER_GROUP, nchunk, kc, MOBA_BLOCK), F32),
                        pltpu.VMEM((HEADS_PER_GROUP, nchunk, kc, MOBA_BLOCK), F32),
                        pltpu.VMEM((2, HEADS_PER_GROUP, 1, MOBA_BLOCK), F32)],
        compiler_params=pltpu.CompilerParams(
            dimension_semantics=("arbitrary", "arbitrary", "arbitrary"),
            vmem_limit_bytes=VMEM_LIMIT),
        name="moba_seq",
    )(q, bias, kt, vt)


def _tail_paged_kernel(pt_ref, x_ref, a_ref, wo_ref, gm_ref, w1_ref, w2_ref, go_ref,
                       qt_ref, knt_ref, vnt_ref, ck_hbm, cv_hbm, y_ref, ot_ref,
                       kbuf, vbuf, s_buf, qb_buf, ksem, vsem, *, fc, n_pages, page, nb, scale):
    b = pl.program_id(0)
    last = pl.num_programs(0) - 1
    HD = qt_ref.shape[0]
    d_ff = w1_ref.shape[1]

    def page_copy(src_hbm, dst, sem, seq, j):
        return pltpu.make_async_copy(src_hbm.at[pt_ref[seq, j]], dst.at[j], sem.at[0])

    def start_pages(src_hbm, dst, sem, seq):
        for j in range(n_pages):
            page_copy(src_hbm, dst, sem, seq, j).start()

    def wait_pages(src_hbm, dst, sem):
        for j in range(n_pages):
            page_copy(src_hbm, dst, sem, b, j).wait()

    def head_sums(x):
        return jnp.sum(x.reshape(N_HEADS, HEAD_DIM, x.shape[1]), axis=1)

    def head_bcast(x):
        return jnp.broadcast_to(x[:, None, :], (N_HEADS, HEAD_DIM, x.shape[1])).reshape(HD, x.shape[1])

    seq_id = lax.broadcasted_iota(jnp.int32, qt_ref.shape, 1)
    column = lambda ref, seq: jnp.sum(jnp.where(seq_id == seq, ref[...], 0.0), axis=1, keepdims=True)
    scaled_q = lambda seq: jnp.broadcast_to(column(qt_ref, seq) * scale, (HD, page))

    @pl.when(b == 0)
    def _():
        start_pages(ck_hbm, kbuf, ksem, b)
        start_pages(cv_hbm, vbuf, vsem, b)
        ot_ref[...] = jnp.zeros(ot_ref.shape, F32)
        qb_buf[...] = scaled_q(b)

    nxt = jnp.minimum(b + 1, last)
    seq_lane = seq_id == b
    wait_pages(ck_hbm, kbuf, ksem)
    for h in range(N_HEADS):
        rows = slice(h * HEAD_DIM, (h + 1) * HEAD_DIM)
        q_h = qb_buf[rows, :]
        for j in range(n_pages):
            s_buf[j, h:h + 1, :] = jnp.sum(kbuf[j, rows, :] * q_h, axis=0, keepdims=True)
    q_col = qb_buf[:, 0:1]
    x3 = x_ref[...] + _dot(a_ref[...].astype(BF16), wo_ref[...])
    xn = _rmsnorm(x3, gm_ref[...]).astype(BF16)
    half = (d_ff // fc // 2) * fc
    mlp = _sqrelu_mlp(xn, w1_ref, w2_ref, fc, 0, half)
    start_pages(ck_hbm, kbuf, ksem, nxt)
    kn_col = column(knt_ref, b)
    vn_col = column(vnt_ref, b)
    qb_buf[...] = scaled_q(nxt)

    ppb = MOBA_BLOCK // page
    blk_id = lax.broadcasted_iota(jnp.int32, (N_HEADS, nb), 1)
    gate = jnp.zeros((N_HEADS, nb), F32)
    for k in range(nb):
        blk = s_buf[k * ppb]
        for j in range(k * ppb + 1, (k + 1) * ppb):
            blk = blk + s_buf[j]
        gate = jnp.where(blk_id == k, jnp.sum(blk, axis=1, keepdims=True) * (1.0 / MOBA_BLOCK), gate)
    sel = _topk_select(gate, jnp.ones(gate.shape, jnp.bool_))
    s_own = head_sums(q_col * kn_col)
    m = s_own
    for j in range(n_pages):
        sj = jnp.where(sel[:, j // ppb:j // ppb + 1], s_buf[j], NEG)
        s_buf[j] = sj
        m = jnp.maximum(m, jnp.max(sj, axis=1, keepdims=True))
    p_own = jnp.exp(s_own - m)
    l = p_own
    for j in range(n_pages):
        pj = jnp.exp(s_buf[j] - m)
        s_buf[j] = pj
        l = l + jnp.sum(pj, axis=1, keepdims=True)

    wait_pages(cv_hbm, vbuf, vsem)
    o_heads = []
    for h in range(N_HEADS):
        rows = slice(h * HEAD_DIM, (h + 1) * HEAD_DIM)
        acc = vbuf[0, rows, :] * s_buf[0, h:h + 1, :]
        for j in range(1, n_pages):
            acc = acc + vbuf[j, rows, :] * s_buf[j, h:h + 1, :]
        o_heads.append(jnp.sum(acc, axis=1, keepdims=True))
    x4 = x3 + mlp + _sqrelu_mlp(xn, w1_ref, w2_ref, fc, half, d_ff)
    y_ref[...] = _rmsnorm(x4, go_ref[...])
    start_pages(cv_hbm, vbuf, vsem, nxt)
    o_col = jnp.concatenate(o_heads, axis=0) + head_bcast(p_own) * vn_col
    o_col = o_col / head_bcast(l)
    ot_ref[...] = jnp.where(seq_lane, o_col, ot_ref[...])

    @pl.when(b == last)
    def _():
        wait_pages(ck_hbm, kbuf, ksem)
        wait_pages(cv_hbm, vbuf, vsem)


def _tail_paged(x, attn, wo, gm, w1, w2, go, qt, knt, vnt, ck, cv, page_table, *, fc=512):
    N, D = x.shape
    HD, NS = qt.shape
    page = ck.shape[2]
    n_pages = page_table.shape[1]
    past_len = n_pages * page
    assert past_len % MOBA_BLOCK == 0 and MOBA_BLOCK % page == 0 and page % LANES == 0
    nb = past_len // MOBA_BLOCK
    assert nb >= 1 and N % NS == 0 and (N // NS) % SUBLANES == 0
    tm = N // NS
    row = lambda v: v.reshape(1, -1)
    weights = (wo, row(gm), w1, w2, row(go))
    full = lambda a: pl.BlockSpec(a.shape, lambda b, pt: (0,) * a.ndim)
    tile = lambda w: pl.BlockSpec((tm, w), lambda b, pt: (b, 0))
    cols = pl.BlockSpec((HD, NS), lambda b, pt: (0, 0))
    return pl.pallas_call(
        functools.partial(_tail_paged_kernel, fc=fc, n_pages=n_pages, page=page, nb=nb,
                          scale=HEAD_DIM ** -0.5),
        grid_spec=pltpu.PrefetchScalarGridSpec(
            num_scalar_prefetch=1,
            grid=(NS,),
            in_specs=[tile(D), tile(attn.shape[1])] + [full(a) for a in weights] + [cols, cols, cols,
                      pl.BlockSpec(memory_space=pl.ANY),
                      pl.BlockSpec(memory_space=pl.ANY)],
            out_specs=[tile(D), cols],
            scratch_shapes=[pltpu.VMEM((n_pages, HD, page), F32),
                            pltpu.VMEM((n_pages, HD, page), F32),
                            pltpu.VMEM((n_pages, N_HEADS, page), F32),
                            pltpu.VMEM((HD, page), F32),
                            pltpu.SemaphoreType.DMA((1,)),
                            pltpu.SemaphoreType.DMA((1,))]),
        out_shape=[jax.ShapeDtypeStruct((N, D), F32), jax.ShapeDtypeStruct((HD, NS), F32)],
        compiler_params=pltpu.CompilerParams(
            dimension_semantics=("arbitrary",), vmem_limit_bytes=VMEM_LIMIT),
        name="tail_paged",
    )(page_table, x, attn, *weights, qt, knt, vnt, ck, cv)


def kernel(x_prompt, x_sample, state_conv, state_h, cache_k, cache_v, page_table, norm_mix, norm_mlp, w_ff1, w_ff2, w_rg_in, b_rg_in, conv_w, conv_b, w_gate_a, b_gate_a, w_gate_i, b_gate_i, lru_lambda, w_rg_out, b_rg_out, norm_kv, w_kv, w_q, w_o, norm_out):
    B, T, D = x_prompt.shape
    NS, TS, _ = x_sample.shape
    depth = norm_mix.shape[0]
    assert depth == 2 and w_rg_in.shape[0] == 1 and w_q.shape[0] == 1 and TS == 1
    HD = N_HEADS * HEAD_DIM
    d_rnn = w_rg_in.shape[2] // 2
    hist = CONV_W - 1
    n_phys, page = cache_k.shape[0], cache_k.shape[1]
    bf = lambda w: w.astype(BF16)

    rg = (norm_mix[0], bf(w_rg_in[0]), b_rg_in[0], conv_w[0], conv_b[0],
          bf(w_gate_a[0]), b_gate_a[0].reshape(-1), bf(w_gate_i[0]), b_gate_i[0].reshape(-1),
          lru_lambda[0], bf(w_rg_out[0]), b_rg_out[0])
    mlp0 = (norm_mlp[0], bf(w_ff1[0]), bf(w_ff2[0]), norm_kv, bf(w_kv.T), norm_mix[1], bf(w_q[0]))
    mlp1 = (bf(w_o[0]), norm_mlp[1], bf(w_ff1[1]), bf(w_ff2[1]), norm_out)

    def heads_last(xt):
        lead = xt.shape[:-2]
        xt = xt.reshape(lead + (N_HEADS, HEAD_DIM, xt.shape[-1]))
        return jnp.moveaxis(xt, -1, -3)

    x1, conv_p, h_p = _rglru_seq(x_prompt, jnp.zeros((B, hist, d_rnn), F32),
                                 jnp.zeros((B, d_rnn), F32), *rg, tm=256)
    x2, kt_p, vt_p, q_p = _mlp_kv(x1, B, *mlp0, tm=256)
    attn_p = _moba_seq(q_p, _moba_gate(q_p, kt_p, tq=min(T, 2048)), kt_p, vt_p)

    xs = x_sample.reshape(NS, D)
    x1s, conv_s, h_s = _rglru_step(xs, jnp.swapaxes(state_conv[0], 0, 1), state_h[0], *rg)
    x2s, kt_s, vt_s, q_s = _mlp_kv(x1s, 1, *mlp0, tm=NS)
    ck = jnp.transpose(cache_k, (0, 2, 3, 1)).reshape(n_phys, HD, page)
    cv = jnp.transpose(cache_v, (0, 2, 3, 1)).reshape(n_phys, HD, page)
    y_p, attn_st = _tail_paged(x2, attn_p, *mlp1, q_s.T, kt_s[0], vt_s[0], ck, cv, page_table)
    y_s = _attn_out_mlp(x2s, attn_st.T, *mlp1, tm=NS)

    return (y_p.reshape(B, T, D), y_s.reshape(NS, 1, D),
            conv_p[None], h_p[None], heads_last(kt_p), heads_last(vt_p),
            jnp.swapaxes(conv_s, 0, 1)[None], h_s[None],
            heads_last(kt_s[0])[:, None], heads_last(vt_s[0])[:, None])
```

```python
import functools
import math

import jax
import jax.numpy as jnp
from jax import lax
from jax.experimental import pallas as pl
from jax.experimental.pallas import tpu as pltpu

N_HEADS = 16
HEAD_DIM = 64
N_LRU_BLOCKS = 4
CONV_W = 4
LRU_C = 8.0
MOBA_BLOCK = 256
MOBA_TOPK = 3
EPS = 1e-6
NEG = -1e30
LOG2E = math.log2(math.e)

LANES = 128
SUBLANES = 8
HEADS_PER_GROUP = LANES // HEAD_DIM
VMEM_LIMIT = 56 * 1024 * 1024

BF16 = jnp.bfloat16
F32 = jnp.float32


def _rmsnorm(x, g):
    return x * lax.rsqrt(jnp.mean(x * x, axis=-1, keepdims=True) + EPS) * g


def _rmsnorm_mxu(x, g):
    d = x.shape[1]
    ss = jnp.dot((x * x).astype(BF16), jnp.ones((d, LANES), BF16), preferred_element_type=F32)
    inv = lax.rsqrt(ss * (1.0 / d) + EPS)
    return x * jnp.concatenate([inv] * (d // LANES), axis=1) * g


def _dot(a, b, precision=None):
    return jnp.dot(a, b, preferred_element_type=F32, precision=precision)


def _dot_nt(a, b):
    return lax.dot_general(a, b, (((1,), (1,)), ((), ())), preferred_element_type=F32)


def _gelu_tanh(x):
    c = math.sqrt(2.0 / math.pi)
    return x * (0.5 * (1.0 + jnp.tanh(c * (x + 0.044715 * (x * x * x)))))


def _softplus(y):
    return jnp.maximum(y, 0.0) + jnp.log1p(jnp.exp(-jnp.abs(y)))


def _neg_expm1(x):
    return jnp.tanh(-0.5 * x) * (jnp.exp(x) + 1.0)


def _lru_coeffs(c, wga_ref, bga, wgi_ref, bgi, lam):
    blk = c.shape[1] // N_LRU_BLOCKS
    cb = c.astype(BF16)
    ra, ri = [], []
    for n in range(N_LRU_BLOCKS):
        cn = cb[:, n * blk:(n + 1) * blk]
        ra.append(_dot(cn, wga_ref[n]))
        ri.append(_dot(cn, wgi_ref[n]))
    r = jax.nn.sigmoid(jnp.concatenate(ra, axis=1) + bga)
    i = jax.nn.sigmoid(jnp.concatenate(ri, axis=1) + bgi)
    log_a = -LRU_C * r * _softplus(-lam)
    a = jnp.exp(log_a)
    mult = jnp.sqrt(_neg_expm1(2.0 * log_a))
    return a, mult * (i * c)


def _rglru_seq_kernel(x_ref, conv0_ref, h0_ref, g_ref, win_ref, bin_ref, cw_ref, cb_ref,
                      wga_ref, bga_ref, wgi_ref, bgi_ref, lam_ref, wout_ref, bout_ref,
                      x1_ref, convn_ref, hn_ref,
                      ubuf, aloc, hloc, hcar, *, tm, d_rnn):
    t = pl.program_id(1)
    nt = pl.num_programs(1)
    hist = CONV_W - 1

    @pl.when(t == 0)
    def _():
        ubuf[SUBLANES - hist:SUBLANES, :] = conv0_ref[0]
        hcar[...] = h0_ref[0]

    x = x_ref[...]
    xn = _rmsnorm_mxu(x, g_ref[...]).astype(BF16)
    proj = _dot(xn, win_ref[...]) + bin_ref[...]
    gate = _gelu_tanh(proj[:, :d_rnn])
    u = proj[:, d_rnn:]
    ubuf[SUBLANES:SUBLANES + tm, :] = u

    c = cb_ref[...] + ubuf[SUBLANES - hist:SUBLANES - hist + tm, :] * cw_ref[0:1, :]
    for k in range(1, CONV_W):
        off = SUBLANES - hist + k
        c = c + ubuf[off:off + tm, :] * cw_ref[k:k + 1, :]

    a, b = _lru_coeffs(c, wga_ref, bga_ref[...], wgi_ref, bgi_ref[...], lam_ref[...])

    a = a.reshape(tm // SUBLANES, SUBLANES, d_rnn)
    b = b.reshape(tm // SUBLANES, SUBLANES, d_rnn)
    row = lax.broadcasted_iota(jnp.int32, a.shape, 1)
    for s in (1, 2, 4):
        ok = row >= s
        a_sh = pltpu.roll(a, s, 1)
        b_sh = pltpu.roll(b, s, 1)
        b = jnp.where(ok, a * b_sh + b, b)
        a = jnp.where(ok, a * a_sh, a)
    aloc[...] = a.reshape(tm, d_rnn)
    hloc[...] = b.reshape(tm, d_rnn)

    def body(gi, carry):
        r0 = pl.multiple_of(gi * SUBLANES, SUBLANES)
        h = hloc[pl.ds(r0, SUBLANES), :] + aloc[pl.ds(r0, SUBLANES), :] * carry
        hloc[pl.ds(r0, SUBLANES), :] = h
        return h[SUBLANES - 1:SUBLANES, :]

    hcar[...] = lax.fori_loop(0, tm // SUBLANES, body, hcar[...])

    hg = (hloc[...] * gate).astype(BF16)
    x1_ref[...] = x + _dot(hg, wout_ref[...]) + bout_ref[...]

    ubuf[SUBLANES - hist:SUBLANES, :] = ubuf[SUBLANES + tm - hist:SUBLANES + tm, :]

    @pl.when(t == nt - 1)
    def _():
        convn_ref[0] = ubuf[SUBLANES - hist:SUBLANES, :]
        hn_ref[0] = hcar[...]


def _rglru_seq(x, conv0, h0, g, w_in, b_in, cw, cb, wga, bga, wgi, bgi, lam, w_out, b_out, *, tm):
    B, T, D = x.shape
    d_rnn = w_in.shape[1] // 2
    hist = CONV_W - 1
    assert T % tm == 0 and tm % SUBLANES == 0 and tm >= hist
    nt = T // tm
    x2 = x.reshape(B * T, D)
    full = lambda a: pl.BlockSpec(a.shape, lambda b, t: (0,) * a.ndim)
    row = lambda v: v.reshape(1, -1)
    args = (x2, conv0, h0.reshape(B, 1, d_rnn), row(g), w_in, row(b_in), cw, row(cb),
            wga, row(bga), wgi, row(bgi), row(lam), w_out, row(b_out))
    in_specs = [pl.BlockSpec((tm, D), lambda b, t: (b * nt + t, 0)),
                pl.BlockSpec((1, hist, d_rnn), lambda b, t: (b, 0, 0)),
                pl.BlockSpec((1, 1, d_rnn), lambda b, t: (b, 0, 0))]
    in_specs += [full(a) for a in args[3:]]
    x1, convn, hn = pl.pallas_call(
        functools.partial(_rglru_seq_kernel, tm=tm, d_rnn=d_rnn),
        grid=(B, nt),
        in_specs=in_specs,
        out_specs=[pl.BlockSpec((tm, D), lambda b, t: (b * nt + t, 0)),
                   pl.BlockSpec((1, hist, d_rnn), lambda b, t: (b, 0, 0)),
                   pl.BlockSpec((1, 1, d_rnn), lambda b, t: (b, 0, 0))],
        out_shape=[jax.ShapeDtypeStruct((B * T, D), F32),
                   jax.ShapeDtypeStruct((B, hist, d_rnn), F32),
                   jax.ShapeDtypeStruct((B, 1, d_rnn), F32)],
        scratch_shapes=[pltpu.VMEM((tm + SUBLANES, d_rnn), F32),
                        pltpu.VMEM((tm, d_rnn), F32),
                        pltpu.VMEM((tm, d_rnn), F32),
                        pltpu.VMEM((1, d_rnn), F32)],
        compiler_params=pltpu.CompilerParams(
            dimension_semantics=("arbitrary", "arbitrary"), vmem_limit_bytes=VMEM_LIMIT),
        name="rglru_seq",
    )(*args)
    return x1, convn, hn.reshape(B, d_rnn)


def _rglru_step_kernel(x_ref, conv_ref, h0_ref, g_ref, win_ref, bin_ref, cw_ref, cb_ref,
                       wga_ref, bga_ref, wgi_ref, bgi_ref, lam_ref, wout_ref, bout_ref,
                       x1_ref, convn_ref, hn_ref, *, d_rnn):
    hist = CONV_W - 1
    x = x_ref[...]
    xn = _rmsnorm(x, g_ref[...]).astype(BF16)
    proj = _dot(xn, win_ref[...]) + bin_ref[...]
    gate = _gelu_tanh(proj[:, :d_rnn])
    u = proj[:, d_rnn:]
    c = cb_ref[...] + conv_ref[0] * cw_ref[0:1, :]
    for k in range(1, hist):
        c = c + conv_ref[k] * cw_ref[k:k + 1, :]
    c = c + u * cw_ref[hist:hist + 1, :]
    a, b = _lru_coeffs(c, wga_ref, bga_ref[...], wgi_ref, bgi_ref[...], lam_ref[...])
    h = a * h0_ref[...] + b
    hn_ref[...] = h
    for k in range(hist - 1):
        convn_ref[k] = conv_ref[k + 1]
    convn_ref[hist - 1] = u
    x1_ref[...] = x + _dot((h * gate).astype(BF16), wout_ref[...]) + bout_ref[...]


def _rglru_step(x, conv0, h0, g, w_in, b_in, cw, cb, wga, bga, wgi, bgi, lam, w_out, b_out):
    N, D = x.shape
    d_rnn = w_in.shape[1] // 2
    hist = CONV_W - 1
    row = lambda v: v.reshape(1, -1)
    args = (x, conv0, h0, row(g), w_in, row(b_in), cw, row(cb),
            wga, row(bga), wgi, row(bgi), row(lam), w_out, row(b_out))
    return pl.pallas_call(
        functools.partial(_rglru_step_kernel, d_rnn=d_rnn),
        out_shape=[jax.ShapeDtypeStruct((N, D), F32),
                   jax.ShapeDtypeStruct((hist, N, d_rnn), F32),
                   jax.ShapeDtypeStruct((N, d_rnn), F32)],
        compiler_params=pltpu.CompilerParams(vmem_limit_bytes=VMEM_LIMIT),
        name="rglru_step",
    )(*args)


def _sqrelu_mlp(xn_bf, w1_ref, w2_ref, fc, lo=0, hi=None):
    hi = w1_ref.shape[1] if hi is None else hi
    acc = None
    for c0 in range(lo, hi, fc):
        h = jnp.maximum(_dot(xn_bf, w1_ref[:, c0:c0 + fc]), 0.0)
        part = _dot((h * h).astype(BF16), w2_ref[c0:c0 + fc, :])
        acc = part if acc is None else acc + part
    return acc


def _mlp_kv_kernel(x_ref, gm_ref, w1_ref, w2_ref, gkv_ref, wkvt_ref, gq_ref, wq_ref,
                   x2_ref, kt_ref, vt_ref, q_ref, *, fc):
    x = x_ref[...]
    x2 = x + _sqrelu_mlp(_rmsnorm(x, gm_ref[...]).astype(BF16), w1_ref, w2_ref, fc)
    x2_ref[...] = x2
    hd = kt_ref.shape[1]
    kvt = _dot_nt(wkvt_ref[...], _rmsnorm(x2, gkv_ref[...]).astype(BF16))
    kt_ref[0] = kvt[:hd, :]
    vt_ref[0] = kvt[hd:, :]
    q_ref[...] = _dot(_rmsnorm(x2, gq_ref[...]).astype(BF16), wq_ref[...])


def _mlp_kv(x, B, gm, w1, w2, gkv, wkvt, gq, wq, *, tm, fc=512):
    N, D = x.shape
    hd = wq.shape[1]
    T = N // B
    assert N == B * T and T % tm == 0
    nt = T // tm
    row = lambda v: v.reshape(1, -1)
    args = (x, row(gm), w1, w2, row(gkv), wkvt, row(gq), wq)
    full = lambda a: pl.BlockSpec(a.shape, lambda i: (0,) * a.ndim)
    tile = lambda w: pl.BlockSpec((tm, w), lambda i: (i, 0))
    tile_t = pl.BlockSpec((1, hd, tm), lambda i: (i // nt, 0, i % nt))
    return pl.pallas_call(
        functools.partial(_mlp_kv_kernel, fc=fc),
        grid=(N // tm,),
        in_specs=[tile(D)] + [full(a) for a in args[1:]],
        out_specs=[tile(D), tile_t, tile_t, tile(hd)],
        out_shape=[jax.ShapeDtypeStruct((N, D), F32),
                   jax.ShapeDtypeStruct((B, hd, T), F32),
                   jax.ShapeDtypeStruct((B, hd, T), F32),
                   jax.ShapeDtypeStruct((N, hd), F32)],
        compiler_params=pltpu.CompilerParams(
            dimension_semantics=("arbitrary",), vmem_limit_bytes=VMEM_LIMIT),
        name="mlp_kv",
    )(*args)


def _attn_out_mlp_kernel(x_ref, a_ref, wo_ref, gm_ref, w1_ref, w2_ref, go_ref, y_ref, *, fc):
    x3 = x_ref[...] + _dot(a_ref[...].astype(BF16), wo_ref[...])
    x4 = x3 + _sqrelu_mlp(_rmsnorm(x3, gm_ref[...]).astype(BF16), w1_ref, w2_ref, fc)
    y_ref[...] = _rmsnorm(x4, go_ref[...])


def _attn_out_mlp(x, attn, wo, gm, w1, w2, go, *, tm, fc=512):
    N, D = x.shape
    assert N % tm == 0
    row = lambda v: v.reshape(1, -1)
    args = (x, attn, wo, row(gm), w1, w2, row(go))
    full = lambda a: pl.BlockSpec(a.shape, lambda i: (0,) * a.ndim)
    tile = lambda w: pl.BlockSpec((tm, w), lambda i: (i, 0))
    return pl.pallas_call(
        functools.partial(_attn_out_mlp_kernel, fc=fc),
        grid=(N // tm,),
        in_specs=[tile(D), tile(attn.shape[1])] + [full(a) for a in args[2:]],
        out_specs=tile(D),
        out_shape=jax.ShapeDtypeStruct((N, D), F32),
        compiler_params=pltpu.CompilerParams(
            dimension_semantics=("arbitrary",), vmem_limit_bytes=VMEM_LIMIT),
        name="attn_out_mlp",
    )(*args)


def _topk_select(gate, allowed, axis=1):
    nb = gate.shape[axis]
    blk = lax.broadcasted_iota(jnp.int32, gate.shape, axis)
    g = jnp.where(allowed, gate, NEG)
    sel = jnp.zeros(gate.shape, jnp.bool_)
    for _ in range(MOBA_TOPK):
        m = jnp.max(g, axis=axis, keepdims=True)
        first = jnp.min(jnp.where(g == m, blk, nb), axis=axis, keepdims=True)
        pick = blk == first
        sel = jnp.logical_or(sel, pick)
        g = jnp.where(pick, -jnp.inf, g)
    return jnp.logical_and(sel, allowed)


def _block_sums(x, nb):
    blk = lax.broadcasted_iota(jnp.int32, (x.shape[0], nb), 1)
    out = jnp.zeros((x.shape[0], nb), F32)
    for j in range(nb):
        sj = jnp.sum(x[:, j * MOBA_BLOCK:(j + 1) * MOBA_BLOCK], axis=1, keepdims=True)
        out = jnp.where(blk == j, sj, out)
    return out


def _moba_gate_kernel(q_ref, kt_ref, bias_ref, kw, *, nb, nbp, tq):
    t = pl.program_id(2)
    blk_rows = MOBA_BLOCK

    @pl.when(t == 0)
    def _():
        km_t = _block_sums(kt_ref[0], nb) * (1.0 / blk_rows)
        km_t = jnp.concatenate([km_t, jnp.zeros((LANES, LANES - nb), F32)], axis=1)
        km = km_t.T[0:nbp, :]
        feat = lax.broadcasted_iota(jnp.int32, km.shape, 1)
        for hh in range(HEADS_PER_GROUP):
            own = jnp.logical_and(feat >= hh * HEAD_DIM, feat < (hh + 1) * HEAD_DIM)
            k_h = jnp.where(own, km, 0.0)
            k_hi = k_h.astype(BF16)
            k_lo = (k_h - k_hi.astype(F32)).astype(BF16)
            kw[hh, 0:nbp, :] = jnp.concatenate([k_hi, k_hi], axis=1)
            kw[hh, nbp:, :] = jnp.concatenate([k_lo, jnp.zeros_like(k_lo)], axis=1)

    q2 = q_ref[...]
    q_hi = q2.astype(BF16)
    q_lo = (q2 - q_hi.astype(F32)).astype(BF16)
    q_cat = jnp.concatenate([q_hi, q_lo], axis=1)
    blk_t = lax.broadcasted_iota(jnp.int32, (nbp, tq), 0)
    q_blk = (t * tq + lax.broadcasted_iota(jnp.int32, (nbp, tq), 1)) // blk_rows
    for hh in range(HEADS_PER_GROUP):
        parts = _dot_nt(kw[hh], q_cat)
        gate_t = parts[0:nbp, :] + parts[nbp:, :]
        sel_t = _topk_select(gate_t, blk_t < q_blk, axis=0)
        bias_t = jnp.where(jnp.logical_or(sel_t, blk_t == q_blk), 0.0, NEG)
        bias_t = jnp.concatenate([bias_t, jnp.zeros((LANES - nbp, tq), F32)], axis=0)
        bias_ref[hh] = bias_t.T.astype(BF16)


def _moba_gate(q, kt, *, tq=1024):
    N, HD = q.shape
    B, _, T = kt.shape
    assert N == B * T and T % tq == 0 and tq % MOBA_BLOCK == 0 and HD % LANES == 0
    nb = T // MOBA_BLOCK
    nbp = -(-nb // (2 * SUBLANES)) * (2 * SUBLANES)
    assert nbp <= LANES
    nt = T // tq
    return pl.pallas_call(
        functools.partial(_moba_gate_kernel, nb=nb, nbp=nbp, tq=tq),
        grid=(B, HD // LANES, nt),
        in_specs=[pl.BlockSpec((tq, LANES), lambda b, g, t: (b * nt + t, g)),
                  pl.BlockSpec((1, LANES, T), lambda b, g, t: (b, g, 0))],
        out_specs=pl.BlockSpec((HEADS_PER_GROUP, tq, LANES), lambda b, g, t: (g, b * nt + t, 0)),
        out_shape=jax.ShapeDtypeStruct((HD // HEAD_DIM, N, LANES), BF16),
        scratch_shapes=[pltpu.VMEM((HEADS_PER_GROUP, 2 * nbp, 2 * LANES), BF16)],
        compiler_params=pltpu.CompilerParams(
            dimension_semantics=("arbitrary", "arbitrary", "arbitrary"),
            vmem_limit_bytes=VMEM_LIMIT),
        name="moba_gate",
    )(q, kt)


def _moba_seq_kernel(q_ref, bias_ref, kt_ref, vt_ref, o_ref, ktaug, vtaug, sbuf_even, sbuf_odd, msbuf, *,
                     nb, bpc, scale):
    i = pl.program_id(2)
    blk_rows = MOBA_BLOCK
    kc = bpc * blk_rows
    nchunk = ktaug.shape[0]
    c_own = i // bpc
    heads = range(HEADS_PER_GROUP)

    @pl.when(i == 0)
    def _():
        key_blk = lax.broadcasted_iota(jnp.int32, (kc, LANES), 0) // blk_rows
        blk_lane = lax.broadcasted_iota(jnp.int32, (kc, LANES), 1)
        feat_row = lax.broadcasted_iota(jnp.int32, (LANES, kc), 0)
        for c in range(nchunk):
            ktaug[c, :, 0:LANES] = kt_ref[0, :, c * kc:(c + 1) * kc].T.astype(BF16)
            ktaug[c, :, LANES:] = jnp.where(blk_lane == key_blk + c * bpc, 1.0, 0.0).astype(BF16)
            vt_c = vt_ref[0, :, c * kc:(c + 1) * kc]
            for hh in heads:
                own = jnp.logical_and(feat_row >= hh * HEAD_DIM, feat_row < (hh + 1) * HEAD_DIM)
                ones_row = ((hh + 1) % HEADS_PER_GROUP) * HEAD_DIM
                vtaug[hh, c] = jnp.where(own, vt_c, jnp.where(feat_row == ones_row, 1.0, 0.0)).astype(BF16)

    def group_max(s, m):
        sm = jnp.max(s.reshape(kc // SUBLANES, SUBLANES, blk_rows), axis=0)
        return sm if m is None else jnp.maximum(m, sm)

    def pairwise(n, body, carry):
        carry = lax.fori_loop(0, n // 2, lambda t, cr: body(2 * t + 1, body(2 * t, cr)), carry)
        return lax.fori_loop(0, n % 2, lambda _, cr: body(n - 1, cr), carry)

    def make_q_augs():
        q2 = q_ref[...]
        lane = lax.broadcasted_iota(jnp.int32, q2.shape, 1)
        out = []
        for hh in heads:
            in_head = jnp.logical_and(lane >= hh * HEAD_DIM, lane < (hh + 1) * HEAD_DIM)
            qs = (jnp.where(in_head, q2, 0.0) * (scale * LOG2E)).astype(BF16)
            out.append(jnp.concatenate([qs, bias_ref[hh]], axis=1))
        return out

    def own_chunk_scores(s_cur, q_augs):
        k_id = lax.broadcasted_iota(jnp.int32, (kc, blk_rows), 0)
        q_id = lax.broadcasted_iota(jnp.int32, (kc, blk_rows), 1)
        own_off = (i - c_own * bpc) * blk_rows
        keep = jnp.logical_or(k_id - own_off <= q_id,
                              jnp.logical_or(k_id < own_off, k_id >= own_off + blk_rows))
        mruns = []
        for hh in heads:
            s = jnp.where(keep, _dot_nt(ktaug[c_own], q_augs[hh]), NEG)
            s_cur[hh, c_own] = s
            mruns.append(group_max(s, None))
        return tuple(mruns)

    def score_chunk(s_cur, c, q_augs, mruns):
        out = []
        for hh in heads:
            s = _dot_nt(ktaug[c], q_augs[hh])
            s_cur[hh, c] = s
            out.append(group_max(s, mruns[hh]))
        return tuple(out)

    def pv_chunk(s_prv, c, ms, accs):
        return tuple(accs[hh] + _dot(vtaug[hh, c], jnp.exp2(s_prv[hh, c] - ms[hh]).astype(BF16))
                     for hh in heads)

    def store_max(cur, mruns):
        for hh in heads:
            msbuf[cur, hh] = jnp.max(mruns[hh], axis=0, keepdims=True)

    def store_out(accs):
        feat = lax.broadcasted_iota(jnp.int32, (LANES, blk_rows), 0)
        o_t = None
        for hh in heads:
            ones_row = ((hh + 1) % HEADS_PER_GROUP) * HEAD_DIM
            oh = accs[hh] / accs[hh][ones_row:ones_row + 1, :]
            o_t = oh if o_t is None else jnp.where(feat >= hh * HEAD_DIM, oh, o_t)
        o_ref[...] = o_t.T

    zero_acc = (jnp.zeros((LANES, blk_rows), F32),) * HEADS_PER_GROUP

    def step(cur, s_cur, s_prv):
        prv = 1 - cur
        parity = i % 2 == cur

        @pl.when(jnp.logical_and(parity, i == 0))
        def _():
            store_max(cur, own_chunk_scores(s_cur, make_q_augs()))
            o_ref[...] = jnp.zeros(o_ref.shape, F32)

        @pl.when(jnp.logical_and(parity, jnp.logical_and(i > 0, i < nb)))
        def _():
            q_augs = make_q_augs()
            ms = [msbuf[prv, hh] for hh in heads]
            c_last = (i - 1) // bpc
            mruns = own_chunk_scores(s_cur, q_augs)
            accs = pv_chunk(s_prv, c_last, ms, zero_acc)

            def both(c, carry):
                mruns, accs = carry
                return score_chunk(s_cur, c, q_augs, mruns), pv_chunk(s_prv, c, ms, accs)

            mruns, accs = pairwise(c_last, both, (mruns, accs))
            n_extra = jnp.where(i % bpc == 0, 1, 0)
            mruns = lax.fori_loop(0, n_extra, lambda _, m: score_chunk(s_cur, c_last, q_augs, m), mruns)
            store_max(cur, mruns)
            store_out(accs)

        @pl.when(jnp.logical_and(parity, i == nb))
        def _():
            ms = [msbuf[prv, hh] for hh in heads]
            store_out(pairwise(nchunk, lambda c, a: pv_chunk(s_prv, c, ms, a), zero_acc))

    step(0, sbuf_even, sbuf_odd)
    step(1, sbuf_odd, sbuf_even)


def _moba_seq(q, bias, kt, vt, *, bpc=4):
    N, HD = q.shape
    B, _, T = kt.shape
    assert N == B * T and T % (bpc * MOBA_BLOCK) == 0 and HD % LANES == 0
    assert HEADS_PER_GROUP >= 2
    nb = T // MOBA_BLOCK
    nchunk = nb // bpc
    kc = bpc * MOBA_BLOCK
    q_blk = lambda b, g, i: (b * nb + jnp.minimum(i, nb - 1), g)
    o_blk = lambda b, g, i: (b * nb + jnp.maximum(i - 1, 0), g)
    resident = dict(pipeline_mode=pl.Buffered(1))
    return pl.pallas_call(
        functools.partial(_moba_seq_kernel, nb=nb, bpc=bpc, scale=HEAD_DIM ** -0.5),
        grid=(B, HD // LANES, nb + 1),
        in_specs=[pl.BlockSpec((MOBA_BLOCK, LANES), q_blk),
                  pl.BlockSpec((HEADS_PER_GROUP, MOBA_BLOCK, LANES),
                               lambda b, g, i: (g, b * nb + jnp.minimum(i, nb - 1), 0)),
                  pl.BlockSpec((1, LANES, T), lambda b, g, i: (b, g, 0), **resident),
                  pl.BlockSpec((1, LANES, T), lambda b, g, i: (b, g, 0), **resident)],
        out_specs=pl.BlockSpec((MOBA_BLOCK, LANES), o_blk),
        out_shape=jax.ShapeDtypeStruct((N, HD), F32),
        scratch_shapes=[pltpu.VMEM((nchunk, kc, 2 * LANES), BF16),
                        pltpu.VMEM((HEADS_PER_GROUP, nchunk, LANES, kc), BF16),
                        pltpu.VMEM((HEADS_PER_GROUP, nchunk, kc, MOBA_BLOCK), F32),
                        pltpu.VMEM((HEADS_PER_GROUP, nchunk, kc, MOBA_BLOCK), F32),
                        pltpu.VMEM((2, HEADS_PER_GROUP, 1, MOBA_BLOCK), F32)],
        compiler_params=pltpu.CompilerParams(
            dimension_semantics=("arbitrary", "arbitrary", "arbitrary"),
            vmem_limit_bytes=VMEM_LIMIT),
        name="moba_seq",
    )(q, bias, kt, vt)


def _tail_paged_kernel(pt_ref, x_ref, a_ref, wo_ref, gm_ref, w1_ref, w2_ref, go_ref,
                       qt_ref, knt_ref, vnt_ref, ck_hbm, cv_hbm, y_ref, ot_ref,
                       kbuf, vbuf, s_buf, qb_buf, ksem, vsem, *, fc, n_pages, page, nb, scale):
    b = pl.program_id(0)
    last = pl.num_programs(0) - 1
    HD = qt_ref.shape[0]
    d_ff = w1_ref.shape[1]

    def page_copy(src_hbm, dst, sem, seq, j):
        return pltpu.make_async_copy(src_hbm.at[pt_ref[seq, j]], dst.at[j], sem.at[0])

    def start_pages(src_hbm, dst, sem, seq):
        for j in range(n_pages):
            page_copy(src_hbm, dst, sem, seq, j).start()

    def wait_pages(src_hbm, dst, sem):
        for j in range(n_pages):
            page_copy(src_hbm, dst, sem, b, j).wait()

    def head_sums(x):
        return jnp.sum(x.reshape(N_HEADS, HEAD_DIM, x.shape[1]), axis=1)

    def head_bcast(x):
        return jnp.broadcast_to(x[:, None, :], (N_HEADS, HEAD_DIM, x.shape[1])).reshape(HD, x.shape[1])

    seq_id = lax.broadcasted_iota(jnp.int32, qt_ref.shape, 1)
    column = lambda ref, seq: jnp.sum(jnp.where(seq_id == seq, ref[...], 0.0), axis=1, keepdims=True)
    scaled_q = lambda seq: jnp.broadcast_to(column(qt_ref, seq) * scale, (HD, page))

    @pl.when(b == 0)
    def _():
        start_pages(ck_hbm, kbuf, ksem, b)
        start_pages(cv_hbm, vbuf, vsem, b)
        ot_ref[...] = jnp.zeros(ot_ref.shape, F32)
        qb_buf[...] = scaled_q(b)

    nxt = jnp.minimum(b + 1, last)
    seq_lane = seq_id == b
    wait_pages(ck_hbm, kbuf, ksem)
    for h in range(N_HEADS):
        rows = slice(h * HEAD_DIM, (h + 1) * HEAD_DIM)
        q_h = qb_buf[rows, :]
        for j in range(n_pages):
            s_buf[j, h:h + 1, :] = jnp.sum(kbuf[j, rows, :] * q_h, axis=0, keepdims=True)
    start_pages(ck_hbm, kbuf, ksem, nxt)
    q_col = qb_buf[:, 0:1]
    x3 = x_ref[...] + _dot(a_ref[...].astype(BF16), wo_ref[...])
    xn = _rmsnorm(x3, gm_ref[...]).astype(BF16)
    half = (d_ff // fc // 2) * fc
    mlp = _sqrelu_mlp(xn, w1_ref, w2_ref, fc, 0, half)
    kn_col = column(knt_ref, b)
    vn_col = column(vnt_ref, b)
    qb_buf[...] = scaled_q(nxt)

    ppb = MOBA_BLOCK // page
    blk_id = lax.broadcasted_iota(jnp.int32, (N_HEADS, nb), 1)
    gate = jnp.zeros((N_HEADS, nb), F32)
    for k in range(nb):
        blk = s_buf[k * ppb]
        for j in range(k * ppb + 1, (k + 1) * ppb):
            blk = blk + s_buf[j]
        gate = jnp.where(blk_id == k, jnp.sum(blk, axis=1, keepdims=True) * (1.0 / MOBA_BLOCK), gate)
    sel = _topk_select(gate, jnp.ones(gate.shape, jnp.bool_))
    s_own = head_sums(q_col * kn_col)
    m = s_own
    for j in range(n_pages):
        sj = jnp.where(sel[:, j // ppb:j // ppb + 1], s_buf[j], NEG)
        s_buf[j] = sj
        m = jnp.maximum(m, jnp.max(sj, axis=1, keepdims=True))
    p_own = jnp.exp(s_own - m)
    l = p_own
    for j in range(n_pages):
        pj = jnp.exp(s_buf[j] - m)
        s_buf[j] = pj
        l = l + jnp.sum(pj, axis=1, keepdims=True)

    wait_pages(cv_hbm, vbuf, vsem)
    o_heads = []
    for h in range(N_HEADS):
        rows = slice(h * HEAD_DIM, (h + 1) * HEAD_DIM)
        acc = vbuf[0, rows, :] * s_buf[0, h:h + 1, :]
        for j in range(1, n_pages):
            acc = acc + vbuf[j, rows, :] * s_buf[j, h:h + 1, :]
        o_heads.append(jnp.sum(acc, axis=1, keepdims=True))
    start_pages(cv_hbm, vbuf, vsem, nxt)
    x4 = x3 + mlp + _sqrelu_mlp(xn, w1_ref, w2_ref, fc, half, d_ff)
    y_ref[...] = _rmsnorm(x4, go_ref[...])
    o_col = jnp.concatenate(o_heads, axis=0) + head_bcast(p_own) * vn_col
    o_col = o_col / head_bcast(l)
    ot_ref[...] = jnp.where(seq_lane, o_col, ot_ref[...])

    @pl.when(b == last)
    def _():
        wait_pages(ck_hbm, kbuf, ksem)
        wait_pages(cv_hbm, vbuf, vsem)


def _tail_paged(x, attn, wo, gm, w1, w2, go, qt, knt, vnt, ck, cv, page_table, *, fc=512):
    N, D = x.shape
    HD, NS = qt.shape
    page = ck.shape[2]
    n_pages = page_table.shape[1]
    past_len = n_pages * page
    assert past_len % MOBA_BLOCK == 0 and MOBA_BLOCK % page == 0 and page % LANES == 0
    nb = past_len // MOBA_BLOCK
    assert nb >= 1 and N % NS == 0 and (N // NS) % SUBLANES == 0
    tm = N // NS
    row = lambda v: v.reshape(1, -1)
    weights = (wo, row(gm), w1, w2, row(go))
    full = lambda a: pl.BlockSpec(a.shape, lambda b, pt: (0,) * a.ndim)
    tile = lambda w: pl.BlockSpec((tm, w), lambda b, pt: (b, 0))
    cols = pl.BlockSpec((HD, NS), lambda b, pt: (0, 0))
    return pl.pallas_call(
        functools.partial(_tail_paged_kernel, fc=fc, n_pages=n_pages, page=page, nb=nb,
                          scale=HEAD_DIM ** -0.5),
        grid_spec=pltpu.PrefetchScalarGridSpec(
            num_scalar_prefetch=1,
            grid=(NS,),
            in_specs=[tile(D), tile(attn.shape[1])] + [full(a) for a in weights] + [cols, cols, cols,
                      pl.BlockSpec(memory_space=pl.ANY),
                      pl.BlockSpec(memory_space=pl.ANY)],
            out_specs=[tile(D), cols],
            scratch_shapes=[pltpu.VMEM((n_pages, HD, page), F32),
                            pltpu.VMEM((n_pages, HD, page), F32),
                            pltpu.VMEM((n_pages, N_HEADS, page), F32),
                            pltpu.VMEM((HD, page), F32),
                            pltpu.SemaphoreType.DMA((1,)),
                            pltpu.SemaphoreType.DMA((1,))]),
        out_shape=[jax.ShapeDtypeStruct((N, D), F32), jax.ShapeDtypeStruct((HD, NS), F32)],
        compiler_params=pltpu.CompilerParams(
            dimension_semantics=("arbitrary",), vmem_limit_bytes=VMEM_LIMIT),
        name="tail_paged",
    )(page_table, x, attn, *weights, qt, knt, vnt, ck, cv)


def kernel(x_prompt, x_sample, state_conv, state_h, cache_k, cache_v, page_table, norm_mix, norm_mlp, w_ff1, w_ff2, w_rg_in, b_rg_in, conv_w, conv_b, w_gate_a, b_gate_a, w_gate_i, b_gate_i, lru_lambda, w_rg_out, b_rg_out, norm_kv, w_kv, w_q, w_o, norm_out):
    B, T, D = x_prompt.shape
    NS, TS, _ = x_sample.shape
    depth = norm_mix.shape[0]
    assert depth == 2 and w_rg_in.shape[0] == 1 and w_q.shape[0] == 1 and TS == 1
    HD = N_HEADS * HEAD_DIM
    d_rnn = w_rg_in.shape[2] // 2
    hist = CONV_W - 1
    n_phys, page = cache_k.shape[0], cache_k.shape[1]
    bf = lambda w: w.astype(BF16)

    rg = (norm_mix[0], bf(w_rg_in[0]), b_rg_in[0], conv_w[0], conv_b[0],
          bf(w_gate_a[0]), b_gate_a[0].reshape(-1), bf(w_gate_i[0]), b_gate_i[0].reshape(-1),
          lru_lambda[0], bf(w_rg_out[0]), b_rg_out[0])
    mlp0 = (norm_mlp[0], bf(w_ff1[0]), bf(w_ff2[0]), norm_kv, bf(w_kv.T), norm_mix[1], bf(w_q[0]))
    mlp1 = (bf(w_o[0]), norm_mlp[1], bf(w_ff1[1]), bf(w_ff2[1]), norm_out)

    def heads_last(xt):
        lead = xt.shape[:-2]
        xt = xt.reshape(lead + (N_HEADS, HEAD_DIM, xt.shape[-1]))
        return jnp.moveaxis(xt, -1, -3)

    x1, conv_p, h_p = _rglru_seq(x_prompt, jnp.zeros((B, hist, d_rnn), F32),
                                 jnp.zeros((B, d_rnn), F32), *rg, tm=256)
    x2, kt_p, vt_p, q_p = _mlp_kv(x1, B, *mlp0, tm=256)
    attn_p = _moba_seq(q_p, _moba_gate(q_p, kt_p, tq=min(T, 2048)), kt_p, vt_p)

    xs = x_sample.reshape(NS, D)
    x1s, conv_s, h_s = _rglru_step(xs, jnp.swapaxes(state_conv[0], 0, 1), state_h[0], *rg)
    x2s, kt_s, vt_s, q_s = _mlp_kv(x1s, 1, *mlp0, tm=NS)
    ck = jnp.transpose(cache_k, (0, 2, 3, 1)).reshape(n_phys, HD, page)
    cv = jnp.transpose(cache_v, (0, 2, 3, 1)).reshape(n_phys, HD, page)
    y_p, attn_st = _tail_paged(x2, attn_p, *mlp1, q_s.T, kt_s[0], vt_s[0], ck, cv, page_table)
    y_s = _attn_out_mlp(x2s, attn_st.T, *mlp1, tm=NS)

    return (y_p.reshape(B, T, D), y_s.reshape(NS, 1, D),
            conv_p[None], h_p[None], heads_last(kt_p), heads_last(vt_p),
            jnp.swapaxes(conv_s, 0, 1)[None], h_s[None],
            heads_last(kt_s[0])[:, None], heads_last(vt_s[0])[:, None])
```

```python
import functools
import math

import jax
import jax.numpy as jnp
from jax import lax
from jax.experimental import pallas as pl
from jax.experimental.pallas import tpu as pltpu

N_HEADS = 16
HEAD_DIM = 64
N_LRU_BLOCKS = 4
CONV_W = 4
LRU_C = 8.0
MOBA_BLOCK = 256
MOBA_TOPK = 3
EPS = 1e-6
NEG = -1e30
LOG2E = math.log2(math.e)

LANES = 128
SUBLANES = 8
HEADS_PER_GROUP = LANES // HEAD_DIM
VMEM_LIMIT = 56 * 1024 * 1024

BF16 = jnp.bfloat16
F32 = jnp.float32


def _rmsnorm(x, g):
    return x * lax.rsqrt(jnp.mean(x * x, axis=-1, keepdims=True) + EPS) * g


def _rmsnorm_mxu(x, g):
    d = x.shape[1]
    ss = jnp.dot((x * x).astype(BF16), jnp.ones((d, LANES), BF16), preferred_element_type=F32)
    inv = lax.rsqrt(ss * (1.0 / d) + EPS)
    return x * jnp.concatenate([inv] * (d // LANES), axis=1) * g


def _dot(a, b, precision=None):
    return jnp.dot(a, b, preferred_element_type=F32, precision=precision)


def _dot_nt(a, b):
    return lax.dot_general(a, b, (((1,), (1,)), ((), ())), preferred_element_type=F32)


def _gelu_tanh(x):
    c = math.sqrt(2.0 / math.pi)
    return x * (0.5 * (1.0 + jnp.tanh(c * (x + 0.044715 * (x * x * x)))))


def _softplus(y):
    return jnp.maximum(y, 0.0) + jnp.log1p(jnp.exp(-jnp.abs(y)))


def _neg_expm1(x):
    return jnp.tanh(-0.5 * x) * (jnp.exp(x) + 1.0)


def _lru_coeffs(c, wga_ref, bga, wgi_ref, bgi, lam):
    blk = c.shape[1] // N_LRU_BLOCKS
    cb = c.astype(BF16)
    ra, ri = [], []
    for n in range(N_LRU_BLOCKS):
        cn = cb[:, n * blk:(n + 1) * blk]
        ra.append(_dot(cn, wga_ref[n]))
        ri.append(_dot(cn, wgi_ref[n]))
    r = jax.nn.sigmoid(jnp.concatenate(ra, axis=1) + bga)
    i = jax.nn.sigmoid(jnp.concatenate(ri, axis=1) + bgi)
    log_a = -LRU_C * r * _softplus(-lam)
    a = jnp.exp(log_a)
    mult = jnp.sqrt(_neg_expm1(2.0 * log_a))
    return a, mult * (i * c)


def _rglru_seq_kernel(x_ref, conv0_ref, h0_ref, g_ref, win_ref, bin_ref, cw_ref, cb_ref,
                      wga_ref, bga_ref, wgi_ref, bgi_ref, lam_ref, wout_ref, bout_ref,
                      x1_ref, convn_ref, hn_ref,
                      ubuf, aloc, hloc, hcar, *, tm, d_rnn):
    t = pl.program_id(1)
    nt = pl.num_programs(1)
    hist = CONV_W - 1

    @pl.when(t == 0)
    def _():
        ubuf[SUBLANES - hist:SUBLANES, :] = conv0_ref[0]
        hcar[...] = h0_ref[0]

    x = x_ref[...]
    xn = _rmsnorm_mxu(x, g_ref[...]).astype(BF16)
    proj = _dot(xn, win_ref[...]) + bin_ref[...]
    gate = _gelu_tanh(proj[:, :d_rnn])
    u = proj[:, d_rnn:]
    ubuf[SUBLANES:SUBLANES + tm, :] = u

    c = cb_ref[...] + ubuf[SUBLANES - hist:SUBLANES - hist + tm, :] * cw_ref[0:1, :]
    for k in range(1, CONV_W):
        off = SUBLANES - hist + k
        c = c + ubuf[off:off + tm, :] * cw_ref[k:k + 1, :]

    a, b = _lru_coeffs(c, wga_ref, bga_ref[...], wgi_ref, bgi_ref[...], lam_ref[...])

    a = a.reshape(tm // SUBLANES, SUBLANES, d_rnn)
    b = b.reshape(tm // SUBLANES, SUBLANES, d_rnn)
    row = lax.broadcasted_iota(jnp.int32, a.shape, 1)
    for s in (1, 2, 4):
        ok = row >= s
        a_sh = pltpu.roll(a, s, 1)
        b_sh = pltpu.roll(b, s, 1)
        b = jnp.where(ok, a * b_sh + b, b)
        a = jnp.where(ok, a * a_sh, a)
    aloc[...] = a.reshape(tm, d_rnn)
    hloc[...] = b.reshape(tm, d_rnn)

    def body(gi, carry):
        r0 = pl.multiple_of(gi * SUBLANES, SUBLANES)
        h = hloc[pl.ds(r0, SUBLANES), :] + aloc[pl.ds(r0, SUBLANES), :] * carry
        hloc[pl.ds(r0, SUBLANES), :] = h
        return h[SUBLANES - 1:SUBLANES, :]

    hcar[...] = lax.fori_loop(0, tm // SUBLANES, body, hcar[...])

    hg = (hloc[...] * gate).astype(BF16)
    x1_ref[...] = x + _dot(hg, wout_ref[...]) + bout_ref[...]

    ubuf[SUBLANES - hist:SUBLANES, :] = ubuf[SUBLANES + tm - hist:SUBLANES + tm, :]

    @pl.when(t == nt - 1)
    def _():
        convn_ref[0] = ubuf[SUBLANES - hist:SUBLANES, :]
        hn_ref[0] = hcar[...]


def _rglru_seq(x, conv0, h0, g, w_in, b_in, cw, cb, wga, bga, wgi, bgi, lam, w_out, b_out, *, tm):
    B, T, D = x.shape
    d_rnn = w_in.shape[1] // 2
    hist = CONV_W - 1
    assert T % tm == 0 and tm % SUBLANES == 0 and tm >= hist
    nt = T // tm
    x2 = x.reshape(B * T, D)
    full = lambda a: pl.BlockSpec(a.shape, lambda b, t: (0,) * a.ndim)
    row = lambda v: v.reshape(1, -1)
    args = (x2, conv0, h0.reshape(B, 1, d_rnn), row(g), w_in, row(b_in), cw, row(cb),
            wga, row(bga), wgi, row(bgi), row(lam), w_out, row(b_out))
    in_specs = [pl.BlockSpec((tm, D), lambda b, t: (b * nt + t, 0)),
                pl.BlockSpec((1, hist, d_rnn), lambda b, t: (b, 0, 0)),
                pl.BlockSpec((1, 1, d_rnn), lambda b, t: (b, 0, 0))]
    in_specs += [full(a) for a in args[3:]]
    x1, convn, hn = pl.pallas_call(
        functools.partial(_rglru_seq_kernel, tm=tm, d_rnn=d_rnn),
        grid=(B, nt),
        in_specs=in_specs,
        out_specs=[pl.BlockSpec((tm, D), lambda b, t: (b * nt + t, 0)),
                   pl.BlockSpec((1, hist, d_rnn), lambda b, t: (b, 0, 0)),
                   pl.BlockSpec((1, 1, d_rnn), lambda b, t: (b, 0, 0))],
        out_shape=[jax.ShapeDtypeStruct((B * T, D), F32),
                   jax.ShapeDtypeStruct((B, hist, d_rnn), F32),
                   jax.ShapeDtypeStruct((B, 1, d_rnn), F32)],
        scratch_shapes=[pltpu.VMEM((tm + SUBLANES, d_rnn), F32),
                        pltpu.VMEM((tm, d_rnn), F32),
                        pltpu.VMEM((tm, d_rnn), F32),
                        pltpu.VMEM((1, d_rnn), F32)],
        compiler_params=pltpu.CompilerParams(
            dimension_semantics=("arbitrary", "arbitrary"), vmem_limit_bytes=VMEM_LIMIT),
        name="rglru_seq",
    )(*args)
    return x1, convn, hn.reshape(B, d_rnn)


def _rglru_step_kernel(x_ref, conv_ref, h0_ref, g_ref, win_ref, bin_ref, cw_ref, cb_ref,
                       wga_ref, bga_ref, wgi_ref, bgi_ref, lam_ref, wout_ref, bout_ref,
                       x1_ref, convn_ref, hn_ref, *, d_rnn):
    hist = CONV_W - 1
    x = x_ref[...]
    xn = _rmsnorm(x, g_ref[...]).astype(BF16)
    proj = _dot(xn, win_ref[...]) + bin_ref[...]
    gate = _gelu_tanh(proj[:, :d_rnn])
    u = proj[:, d_rnn:]
    c = cb_ref[...] + conv_ref[0] * cw_ref[0:1, :]
    for k in range(1, hist):
        c = c + conv_ref[k] * cw_ref[k:k + 1, :]
    c = c + u * cw_ref[hist:hist + 1, :]
    a, b = _lru_coeffs(c, wga_ref, bga_ref[...], wgi_ref, bgi_ref[...], lam_ref[...])
    h = a * h0_ref[...] + b
    hn_ref[...] = h
    for k in range(hist - 1):
        convn_ref[k] = conv_ref[k + 1]
    convn_ref[hist - 1] = u
    x1_ref[...] = x + _dot((h * gate).astype(BF16), wout_ref[...]) + bout_ref[...]


def _rglru_step(x, conv0, h0, g, w_in, b_in, cw, cb, wga, bga, wgi, bgi, lam, w_out, b_out):
    N, D = x.shape
    d_rnn = w_in.shape[1] // 2
    hist = CONV_W - 1
    row = lambda v: v.reshape(1, -1)
    args = (x, conv0, h0, row(g), w_in, row(b_in), cw, row(cb),
            wga, row(bga), wgi, row(bgi), row(lam), w_out, row(b_out))
    return pl.pallas_call(
        functools.partial(_rglru_step_kernel, d_rnn=d_rnn),
        out_shape=[jax.ShapeDtypeStruct((N, D), F32),
                   jax.ShapeDtypeStruct((hist, N, d_rnn), F32),
                   jax.ShapeDtypeStruct((N, d_rnn), F32)],
        compiler_params=pltpu.CompilerParams(vmem_limit_bytes=VMEM_LIMIT),
        name="rglru_step",
    )(*args)


def _sqrelu_mlp(xn_bf, w1_ref, w2_ref, fc, lo=0, hi=None):
    hi = w1_ref.shape[1] if hi is None else hi
    acc = None
    for c0 in range(lo, hi, fc):
        h = jnp.maximum(_dot(xn_bf, w1_ref[:, c0:c0 + fc]), 0.0)
        part = _dot((h * h).astype(BF16), w2_ref[c0:c0 + fc, :])
        acc = part if acc is None else acc + part
    return acc


def _mlp_kv_kernel(x_ref, gm_ref, w1_ref, w2_ref, gkv_ref, wkvt_ref, gq_ref, wq_ref,
                   x2_ref, kt_ref, vt_ref, q_ref, *, fc):
    x = x_ref[...]
    x2 = x + _sqrelu_mlp(_rmsnorm(x, gm_ref[...]).astype(BF16), w1_ref, w2_ref, fc)
    x2_ref[...] = x2
    hd = kt_ref.shape[1]
    kvt = _dot_nt(wkvt_ref[...], _rmsnorm(x2, gkv_ref[...]).astype(BF16))
    kt_ref[0] = kvt[:hd, :]
    vt_ref[0] = kvt[hd:, :]
    q_ref[...] = _dot(_rmsnorm(x2, gq_ref[...]).astype(BF16), wq_ref[...])


def _mlp_kv(x, B, gm, w1, w2, gkv, wkvt, gq, wq, *, tm, fc=512):
    N, D = x.shape
    hd = wq.shape[1]
    T = N // B
    assert N == B * T and T % tm == 0
    nt = T // tm
    row = lambda v: v.reshape(1, -1)
    args = (x, row(gm), w1, w2, row(gkv), wkvt, row(gq), wq)
    full = lambda a: pl.BlockSpec(a.shape, lambda i: (0,) * a.ndim)
    tile = lambda w: pl.BlockSpec((tm, w), lambda i: (i, 0))
    tile_t = pl.BlockSpec((1, hd, tm), lambda i: (i // nt, 0, i % nt))
    return pl.pallas_call(
        functools.partial(_mlp_kv_kernel, fc=fc),
        grid=(N // tm,),
        in_specs=[tile(D)] + [full(a) for a in args[1:]],
        out_specs=[tile(D), tile_t, tile_t, tile(hd)],
        out_shape=[jax.ShapeDtypeStruct((N, D), F32),
                   jax.ShapeDtypeStruct((B, hd, T), F32),
                   jax.ShapeDtypeStruct((B, hd, T), F32),
                   jax.ShapeDtypeStruct((N, hd), F32)],
        compiler_params=pltpu.CompilerParams(
            dimension_semantics=("arbitrary",), vmem_limit_bytes=VMEM_LIMIT),
        name="mlp_kv",
    )(*args)


def _attn_out_mlp_kernel(x_ref, a_ref, wo_ref, gm_ref, w1_ref, w2_ref, go_ref, y_ref, *, fc):
    x3 = x_ref[...] + _dot(a_ref[...].astype(BF16), wo_ref[...])
    x4 = x3 + _sqrelu_mlp(_rmsnorm(x3, gm_ref[...]).astype(BF16), w1_ref, w2_ref, fc)
    y_ref[...] = _rmsnorm(x4, go_ref[...])


def _attn_out_mlp(x, attn, wo, gm, w1, w2, go, *, tm, fc=512):
    N, D = x.shape
    assert N % tm == 0
    row = lambda v: v.reshape(1, -1)
    args = (x, attn, wo, row(gm), w1, w2, row(go))
    full = lambda a: pl.BlockSpec(a.shape, lambda i: (0,) * a.ndim)
    tile = lambda w: pl.BlockSpec((tm, w), lambda i: (i, 0))
    return pl.pallas_call(
        functools.partial(_attn_out_mlp_kernel, fc=fc),
        grid=(N // tm,),
        in_specs=[tile(D), tile(attn.shape[1])] + [full(a) for a in args[2:]],
        out_specs=tile(D),
        out_shape=jax.ShapeDtypeStruct((N, D), F32),
        compiler_params=pltpu.CompilerParams(
            dimension_semantics=("arbitrary",), vmem_limit_bytes=VMEM_LIMIT),
        name="attn_out_mlp",
    )(*args)


def _topk_select(gate, allowed, axis=1):
    nb = gate.shape[axis]
    blk = lax.broadcasted_iota(jnp.int32, gate.shape, axis)
    g = jnp.where(allowed, gate, NEG)
    sel = jnp.zeros(gate.shape, jnp.bool_)
    for _ in range(MOBA_TOPK):
        m = jnp.max(g, axis=axis, keepdims=True)
        first = jnp.min(jnp.where(g == m, blk, nb), axis=axis, keepdims=True)
        pick = blk == first
        sel = jnp.logical_or(sel, pick)
        g = jnp.where(pick, -jnp.inf, g)
    return jnp.logical_and(sel, allowed)


def _block_sums(x, nb):
    blk = lax.broadcasted_iota(jnp.int32, (x.shape[0], nb), 1)
    out = jnp.zeros((x.shape[0], nb), F32)
    for j in range(nb):
        sj = jnp.sum(x[:, j * MOBA_BLOCK:(j + 1) * MOBA_BLOCK], axis=1, keepdims=True)
        out = jnp.where(blk == j, sj, out)
    return out


def _moba_gate_kernel(q_ref, kt_ref, bias_ref, kw, *, nb, nbp, tq):
    t = pl.program_id(2)
    blk_rows = MOBA_BLOCK

    @pl.when(t == 0)
    def _():
        km_t = _block_sums(kt_ref[0], nb) * (1.0 / blk_rows)
        km_t = jnp.concatenate([km_t, jnp.zeros((LANES, LANES - nb), F32)], axis=1)
        km = km_t.T[0:nbp, :]
        feat = lax.broadcasted_iota(jnp.int32, km.shape, 1)
        for hh in range(HEADS_PER_GROUP):
            own = jnp.logical_and(feat >= hh * HEAD_DIM, feat < (hh + 1) * HEAD_DIM)
            k_h = jnp.where(own, km, 0.0)
            k_hi = k_h.astype(BF16)
            k_lo = (k_h - k_hi.astype(F32)).astype(BF16)
            kw[hh, 0:nbp, :] = jnp.concatenate([k_hi, k_hi], axis=1)
            kw[hh, nbp:, :] = jnp.concatenate([k_lo, jnp.zeros_like(k_lo)], axis=1)

    q2 = q_ref[...]
    q_hi = q2.astype(BF16)
    q_lo = (q2 - q_hi.astype(F32)).astype(BF16)
    q_cat = jnp.concatenate([q_hi, q_lo], axis=1)
    blk_t = lax.broadcasted_iota(jnp.int32, (nbp, tq), 0)
    q_blk = (t * tq + lax.broadcasted_iota(jnp.int32, (nbp, tq), 1)) // blk_rows
    for hh in range(HEADS_PER_GROUP):
        parts = _dot_nt(kw[hh], q_cat)
        gate_t = parts[0:nbp, :] + parts[nbp:, :]
        sel_t = _topk_select(gate_t, blk_t < q_blk, axis=0)
        bias_t = jnp.where(jnp.logical_or(sel_t, blk_t == q_blk), 0.0, NEG)
        bias_t = jnp.concatenate([bias_t, jnp.zeros((LANES - nbp, tq), F32)], axis=0)
        bias_ref[hh] = bias_t.T.astype(BF16)


def _moba_gate(q, kt, *, tq=1024):
    N, HD = q.shape
    B, _, T = kt.shape
    assert N == B * T and T % tq == 0 and tq % MOBA_BLOCK == 0 and HD % LANES == 0
    nb = T // MOBA_BLOCK
    nbp = -(-nb // (2 * SUBLANES)) * (2 * SUBLANES)
    assert nbp <= LANES
    nt = T // tq
    return pl.pallas_call(
        functools.partial(_moba_gate_kernel, nb=nb, nbp=nbp, tq=tq),
        grid=(B, HD // LANES, nt),
        in_specs=[pl.BlockSpec((tq, LANES), lambda b, g, t: (b * nt + t, g)),
                  pl.BlockSpec((1, LANES, T), lambda b, g, t: (b, g, 0))],
        out_specs=pl.BlockSpec((HEADS_PER_GROUP, tq, LANES), lambda b, g, t: (g, b * nt + t, 0)),
        out_shape=jax.ShapeDtypeStruct((HD // HEAD_DIM, N, LANES), BF16),
        scratch_shapes=[pltpu.VMEM((HEADS_PER_GROUP, 2 * nbp, 2 * LANES), BF16)],
        compiler_params=pltpu.CompilerParams(
            dimension_semantics=("arbitrary", "arbitrary", "arbitrary"),
            vmem_limit_bytes=VMEM_LIMIT),
        name="moba_gate",
    )(q, kt)


def _moba_seq_kernel(q_ref, bias_ref, kt_ref, vt_ref, o_ref, ktaug, vtaug, sbuf_even, sbuf_odd, msbuf, *,
                     nb, bpc, scale):
    i = pl.program_id(2)
    blk_rows = MOBA_BLOCK
    kc = bpc * blk_rows
    nchunk = ktaug.shape[0]
    c_own = i // bpc
    heads = range(HEADS_PER_GROUP)

    @pl.when(i == 0)
    def _():
        key_blk = lax.broadcasted_iota(jnp.int32, (kc, LANES), 0) // blk_rows
        blk_lane = lax.broadcasted_iota(jnp.int32, (kc, LANES), 1)
        feat_row = lax.broadcasted_iota(jnp.int32, (LANES, kc), 0)
        for c in range(nchunk):
            ktaug[c, :, 0:LANES] = kt_ref[0, :, c * kc:(c + 1) * kc].T.astype(BF16)
            ktaug[c, :, LANES:] = jnp.where(blk_lane == key_blk + c * bpc, 1.0, 0.0).astype(BF16)
            vt_c = vt_ref[0, :, c * kc:(c + 1) * kc]
            for hh in heads:
                own = jnp.logical_and(feat_row >= hh * HEAD_DIM, feat_row < (hh + 1) * HEAD_DIM)
                ones_row = ((hh + 1) % HEADS_PER_GROUP) * HEAD_DIM
                vtaug[hh, c] = jnp.where(own, vt_c, jnp.where(feat_row == ones_row, 1.0, 0.0)).astype(BF16)

    def group_max(s, m):
        sm = jnp.max(s.reshape(kc // SUBLANES, SUBLANES, blk_rows), axis=0)
        return sm if m is None else jnp.maximum(m, sm)

    def pairwise(n, body, carry):
        carry = lax.fori_loop(0, n // 2, lambda t, cr: body(2 * t + 1, body(2 * t, cr)), carry)
        return lax.fori_loop(0, n % 2, lambda _, cr: body(n - 1, cr), carry)

    def make_q_augs():
        q2 = q_ref[...]
        lane = lax.broadcasted_iota(jnp.int32, q2.shape, 1)
        out = []
        for hh in heads:
            in_head = jnp.logical_and(lane >= hh * HEAD_DIM, lane < (hh + 1) * HEAD_DIM)
            qs = (jnp.where(in_head, q2, 0.0) * (scale * LOG2E)).astype(BF16)
            out.append(jnp.concatenate([qs, bias_ref[hh]], axis=1))
        return out

    def own_chunk_scores(s_cur, q_augs):
        k_id = lax.broadcasted_iota(jnp.int32, (kc, blk_rows), 0)
        q_id = lax.broadcasted_iota(jnp.int32, (kc, blk_rows), 1)
        own_off = (i - c_own * bpc) * blk_rows
        keep = jnp.logical_or(k_id - own_off <= q_id,
                              jnp.logical_or(k_id < own_off, k_id >= own_off + blk_rows))
        mruns = []
        for hh in heads:
            s = jnp.where(keep, _dot_nt(ktaug[c_own], q_augs[hh]), NEG)
            s_cur[hh, c_own] = s
            mruns.append(group_max(s, None))
        return tuple(mruns)

    def score_chunk(s_cur, c, q_augs, mruns):
        out = []
        for hh in heads:
            s = _dot_nt(ktaug[c], q_augs[hh])
            s_cur[hh, c] = s
            out.append(group_max(s, mruns[hh]))
        return tuple(out)

    def pv_chunk(s_prv, c, ms, accs):
        return tuple(accs[hh] + _dot(vtaug[hh, c], jnp.exp2(s_prv[hh, c] - ms[hh]).astype(BF16))
                     for hh in heads)

    def store_max(cur, mruns):
        for hh in heads:
            msbuf[cur, hh] = jnp.max(mruns[hh], axis=0, keepdims=True)

    def store_out(accs):
        feat = lax.broadcasted_iota(jnp.int32, (LANES, blk_rows), 0)
        o_t = None
        for hh in heads:
            ones_row = ((hh + 1) % HEADS_PER_GROUP) * HEAD_DIM
            oh = accs[hh] / accs[hh][ones_row:ones_row + 1, :]
            o_t = oh if o_t is None else jnp.where(feat >= hh * HEAD_DIM, oh, o_t)
        o_ref[...] = o_t.T

    zero_acc = (jnp.zeros((LANES, blk_rows), F32),) * HEADS_PER_GROUP

    def step(cur, s_cur, s_prv):
        prv = 1 - cur
        parity = i % 2 == cur

        @pl.when(jnp.logical_and(parity, i == 0))
        def _():
            store_max(cur, own_chunk_scores(s_cur, make_q_augs()))
            o_ref[...] = jnp.zeros(o_ref.shape, F32)

        @pl.when(jnp.logical_and(parity, jnp.logical_and(i > 0, i < nb)))
        def _():
            q_augs = make_q_augs()
            ms = [msbuf[prv, hh] for hh in heads]
            c_last = (i - 1) // bpc
            mruns = own_chunk_scores(s_cur, q_augs)
            accs = pv_chunk(s_prv, c_last, ms, zero_acc)

            def both(c, carry):
                mruns, accs = carry
                return score_chunk(s_cur, c, q_augs, mruns), pv_chunk(s_prv, c, ms, accs)

            mruns, accs = pairwise(c_last, both, (mruns, accs))
            n_extra = jnp.where(i % bpc == 0, 1, 0)
            mruns = lax.fori_loop(0, n_extra, lambda _, m: score_chunk(s_cur, c_last, q_augs, m), mruns)
            store_max(cur, mruns)
            store_out(accs)

        @pl.when(jnp.logical_and(parity, i == nb))
        def _():
            ms = [msbuf[prv, hh] for hh in heads]
            store_out(pairwise(nchunk, lambda c, a: pv_chunk(s_prv, c, ms, a), zero_acc))

    step(0, sbuf_even, sbuf_odd)
    step(1, sbuf_odd, sbuf_even)


def _moba_seq(q, bias, kt, vt, *, bpc=4):
    N, HD = q.shape
    B, _, T = kt.shape
    assert N == B * T and T % (bpc * MOBA_BLOCK) == 0 and HD % LANES == 0
    assert HEADS_PER_GROUP >= 2
    nb = T // MOBA_BLOCK
    nchunk = nb // bpc
    kc = bpc * MOBA_BLOCK
    q_blk = lambda b, g, i: (b * nb + jnp.minimum(i, nb - 1), g)
    o_blk = lambda b, g, i: (b * nb + jnp.maximum(i - 1, 0), g)
    resident = dict(pipeline_mode=pl.Buffered(1))
    return pl.pallas_call(
        functools.partial(_moba_seq_kernel, nb=nb, bpc=bpc, scale=HEAD_DIM ** -0.5),
        grid=(B, HD // LANES, nb + 1),
        in_specs=[pl.BlockSpec((MOBA_BLOCK, LANES), q_blk),
                  pl.BlockSpec((HEADS_PER_GROUP, MOBA_BLOCK, LANES),
                               lambda b, g, i: (g, b * nb + jnp.minimum(i, nb - 1), 0)),
                  pl.BlockSpec((1, LANES, T), lambda b, g, i: (b, g, 0), **resident),
                  pl.BlockSpec((1, LANES, T), lambda b, g, i: (b, g, 0), **resident)],
        out_specs=pl.BlockSpec((MOBA_BLOCK, LANES), o_blk),
        out_shape=jax.ShapeDtypeStruct((N, HD), F32),
        scratch_shapes=[pltpu.VMEM((nchunk, kc, 2 * LANES), BF16),
                        pltpu.VMEM((HEADS_PER_GROUP, nchunk, LANES, kc), BF16),
                        pltpu.VMEM((HEADS_PER_GROUP, nchunk, kc, MOBA_BLOCK), F32),
                        pltpu.VMEM((HEADS_PER_GROUP, nchunk, kc, MOBA_BLOCK), F32),
                        pltpu.VMEM((2, HEADS_PER_GROUP, 1, MOBA_BLOCK), F32)],
        compiler_params=pltpu.CompilerParams(
            dimension_semantics=("arbitrary", "arbitrary", "arbitrary"),
            vmem_limit_bytes=VMEM_LIMIT),
        name="moba_seq",
    )(q, bias, kt, vt)


def _tail_paged_kernel(pt_ref, x_ref, a_ref, wo_ref, gm_ref, w1_ref, w2_ref, go_ref,
                       qt_ref, knt_ref, vnt_ref, ck_hbm, cv_hbm, y_ref, ot_ref,
                       kbuf, vbuf, s_buf, ksem, vsem, *, fc, n_pages, page, nb, scale):
    b = pl.program_id(0)
    last = pl.num_programs(0) - 1
    HD = qt_ref.shape[0]
    d_ff = w1_ref.shape[1]

    def page_copy(src_hbm, dst, sem, seq, j):
        return pltpu.make_async_copy(src_hbm.at[pt_ref[seq, j]], dst.at[j], sem.at[0])

    def start_pages(src_hbm, dst, sem, seq):
        for j in range(n_pages):
            page_copy(src_hbm, dst, sem, seq, j).start(priority=j % 2)

    def wait_pages(src_hbm, dst, sem):
        for j in range(n_pages):
            page_copy(src_hbm, dst, sem, b, j).wait()

    def head_sums(x):
        return jnp.sum(x.reshape(N_HEADS, HEAD_DIM, x.shape[1]), axis=1)

    def head_bcast(x):
        return jnp.broadcast_to(x[:, None, :], (N_HEADS, HEAD_DIM, x.shape[1])).reshape(HD, x.shape[1])

    @pl.when(b == 0)
    def _():
        start_pages(ck_hbm, kbuf, ksem, b)
        start_pages(cv_hbm, vbuf, vsem, b)
        ot_ref[...] = jnp.zeros(ot_ref.shape, F32)

    seq_lane = lax.broadcasted_iota(jnp.int32, qt_ref.shape, 1) == b
    column = lambda ref: jnp.sum(jnp.where(seq_lane, ref[...], 0.0), axis=1, keepdims=True)
    q_col = column(qt_ref) * scale
    kn_col = column(knt_ref)
    vn_col = column(vnt_ref)
    q_b = jnp.broadcast_to(q_col, (HD, page))

    nxt = jnp.minimum(b + 1, last)
    wait_pages(ck_hbm, kbuf, ksem)
    for h in range(N_HEADS):
        rows = slice(h * HEAD_DIM, (h + 1) * HEAD_DIM)
        q_h = q_b[rows, :]
        for j in range(n_pages):
            s_buf[j, h:h + 1, :] = jnp.sum(kbuf[j, rows, :] * q_h, axis=0, keepdims=True)
    x3 = x_ref[...] + _dot(a_ref[...].astype(BF16), wo_ref[...])
    xn = _rmsnorm(x3, gm_ref[...]).astype(BF16)
    half = (d_ff // fc // 2) * fc
    mlp = _sqrelu_mlp(xn, w1_ref, w2_ref, fc, 0, half)
    start_pages(ck_hbm, kbuf, ksem, nxt)

    ppb = MOBA_BLOCK // page
    blk_id = lax.broadcasted_iota(jnp.int32, (N_HEADS, nb), 1)
    gate = jnp.zeros((N_HEADS, nb), F32)
    for k in range(nb):
        blk = s_buf[k * ppb]
        for j in range(k * ppb + 1, (k + 1) * ppb):
            blk = blk + s_buf[j]
        gate = jnp.where(blk_id == k, jnp.sum(blk, axis=1, keepdims=True) * (1.0 / MOBA_BLOCK), gate)
    sel = _topk_select(gate, jnp.ones(gate.shape, jnp.bool_))
    s_own = head_sums(q_col * kn_col)
    m = s_own
    for j in range(n_pages):
        sj = jnp.where(sel[:, j // ppb:j // ppb + 1], s_buf[j], NEG)
        s_buf[j] = sj
        m = jnp.maximum(m, jnp.max(sj, axis=1, keepdims=True))
    p_own = jnp.exp(s_own - m)
    l = p_own
    for j in range(n_pages):
        pj = jnp.exp(s_buf[j] - m)
        s_buf[j] = pj
        l = l + jnp.sum(pj, axis=1, keepdims=True)

    wait_pages(cv_hbm, vbuf, vsem)
    o_heads = []
    for h in range(N_HEADS):
        rows = slice(h * HEAD_DIM, (h + 1) * HEAD_DIM)
        acc = vbuf[0, rows, :] * s_buf[0, h:h + 1, :]
        for j in range(1, n_pages):
            acc = acc + vbuf[j, rows, :] * s_buf[j, h:h + 1, :]
        o_heads.append(jnp.sum(acc, axis=1, keepdims=True))
    x4 = x3 + mlp + _sqrelu_mlp(xn, w1_ref, w2_ref, fc, half, d_ff)
    y_ref[...] = _rmsnorm(x4, go_ref[...])
    start_pages(cv_hbm, vbuf, vsem, nxt)
    o_col = jnp.concatenate(o_heads, axis=0) + head_bcast(p_own) * vn_col
    o_col = o_col / head_bcast(l)
    ot_ref[...] = jnp.where(seq_lane, o_col, ot_ref[...])

    @pl.when(b == last)
    def _():
        wait_pages(ck_hbm, kbuf, ksem)
        wait_pages(cv_hbm, vbuf, vsem)


def _tail_paged(x, attn, wo, gm, w1, w2, go, qt, knt, vnt, ck, cv, page_table, *, fc=512):
    N, D = x.shape
    HD, NS = qt.shape
    page = ck.shape[2]
    n_pages = page_table.shape[1]
    past_len = n_pages * page
    assert past_len % MOBA_BLOCK == 0 and MOBA_BLOCK % page == 0 and page % LANES == 0
    nb = past_len // MOBA_BLOCK
    assert nb >= 1 and N % NS == 0 and (N // NS) % SUBLANES == 0
    tm = N // NS
    row = lambda v: v.reshape(1, -1)
    weights = (wo, row(gm), w1, w2, row(go))
    full = lambda a: pl.BlockSpec(a.shape, lambda b, pt: (0,) * a.ndim)
    tile = lambda w: pl.BlockSpec((tm, w), lambda b, pt: (b, 0))
    cols = pl.BlockSpec((HD, NS), lambda b, pt: (0, 0))
    return pl.pallas_call(
        functools.partial(_tail_paged_kernel, fc=fc, n_pages=n_pages, page=page, nb=nb,
                          scale=HEAD_DIM ** -0.5),
        grid_spec=pltpu.PrefetchScalarGridSpec(
            num_scalar_prefetch=1,
            grid=(NS,),
            in_specs=[tile(D), tile(attn.shape[1])] + [full(a) for a in weights] + [cols, cols, cols,
                      pl.BlockSpec(memory_space=pl.ANY),
                      pl.BlockSpec(memory_space=pl.ANY)],
            out_specs=[tile(D), cols],
            scratch_shapes=[pltpu.VMEM((n_pages, HD, page), F32),
                            pltpu.VMEM((n_pages, HD, page), F32),
                            pltpu.VMEM((n_pages, N_HEADS, page), F32),
                            pltpu.SemaphoreType.DMA((1,)),
                            pltpu.SemaphoreType.DMA((1,))]),
        out_shape=[jax.ShapeDtypeStruct((N, D), F32), jax.ShapeDtypeStruct((HD, NS), F32)],
        compiler_params=pltpu.CompilerParams(
            dimension_semantics=("arbitrary",), vmem_limit_bytes=VMEM_LIMIT),
        name="tail_paged",
    )(page_table, x, attn, *weights, qt, knt, vnt, ck, cv)


def kernel(x_prompt, x_sample, state_conv, state_h, cache_k, cache_v, page_table, norm_mix, norm_mlp, w_ff1, w_ff2, w_rg_in, b_rg_in, conv_w, conv_b, w_gate_a, b_gate_a, w_gate_i, b_gate_i, lru_lambda, w_rg_out, b_rg_out, norm_kv, w_kv, w_q, w_o, norm_out):
    B, T, D = x_prompt.shape
    NS, TS, _ = x_sample.shape
    depth = norm_mix.shape[0]
    assert depth == 2 and w_rg_in.shape[0] == 1 and w_q.shape[0] == 1 and TS == 1
    HD = N_HEADS * HEAD_DIM
    d_rnn = w_rg_in.shape[2] // 2
    hist = CONV_W - 1
    n_phys, page = cache_k.shape[0], cache_k.shape[1]
    bf = lambda w: w.astype(BF16)

    rg = (norm_mix[0], bf(w_rg_in[0]), b_rg_in[0], conv_w[0], conv_b[0],
          bf(w_gate_a[0]), b_gate_a[0].reshape(-1), bf(w_gate_i[0]), b_gate_i[0].reshape(-1),
          lru_lambda[0], bf(w_rg_out[0]), b_rg_out[0])
    mlp0 = (norm_mlp[0], bf(w_ff1[0]), bf(w_ff2[0]), norm_kv, bf(w_kv.T), norm_mix[1], bf(w_q[0]))
    mlp1 = (bf(w_o[0]), norm_mlp[1], bf(w_ff1[1]), bf(w_ff2[1]), norm_out)

    def heads_last(xt):
        lead = xt.shape[:-2]
        xt = xt.reshape(lead + (N_HEADS, HEAD_DIM, xt.shape[-1]))
        return jnp.moveaxis(xt, -1, -3)

    x1, conv_p, h_p = _rglru_seq(x_prompt, jnp.zeros((B, hist, d_rnn), F32),
                                 jnp.zeros((B, d_rnn), F32), *rg, tm=256)
    x2, kt_p, vt_p, q_p = _mlp_kv(x1, B, *mlp0, tm=256)
    attn_p = _moba_seq(q_p, _moba_gate(q_p, kt_p, tq=min(T, 2048)), kt_p, vt_p)

    xs = x_sample.reshape(NS, D)
    x1s, conv_s, h_s = _rglru_step(xs, jnp.swapaxes(state_conv[0], 0, 1), state_h[0], *rg)
    x2s, kt_s, vt_s, q_s = _mlp_kv(x1s, 1, *mlp0, tm=NS)
    ck = jnp.transpose(cache_k, (0, 2, 3, 1)).reshape(n_phys, HD, page)
    cv = jnp.transpose(cache_v, (0, 2, 3, 1)).reshape(n_phys, HD, page)
    y_p, attn_st = _tail_paged(x2, attn_p, *mlp1, q_s.T, kt_s[0], vt_s[0], ck, cv, page_table)
    y_s = _attn_out_mlp(x2s, attn_st.T, *mlp1, tm=NS)

    return (y_p.reshape(B, T, D), y_s.reshape(NS, 1, D),
            conv_p[None], h_p[None], heads_last(kt_p), heads_last(vt_p),
            jnp.swapaxes(conv_s, 0, 1)[None], h_s[None],
            heads_last(kt_s[0])[:, None], heads_last(vt_s[0])[:, None])
```
